```python
import math
import jax, jax.numpy as jnp
from jax import lax
import numpy as np

D_MODEL = 4096
BATCH = 1
SEQ = 16384
DEPTH = 4

ATT_HEAD_DIM = 128
ATT_HEADS = D_MODEL // 512
ATT_WIDTH = ATT_HEADS * 2 * ATT_HEAD_DIM
GROUP_DIM = 128
CONF_GROUPS = D_MODEL // 512
CONF_WIDTH = CONF_GROUPS * GROUP_DIM
CONF_KERNEL = 31
SC_GROUPS = D_MODEL // 512
SC_WIDTH = SC_GROUPS * GROUP_DIM
SC_KERNEL = 3
MIX_WIDTH = ATT_WIDTH + CONF_WIDTH + SC_WIDTH
IN_SPLITS = [ATT_WIDTH] * 3 + [CONF_WIDTH] * 2 + [SC_WIDTH] * 3
IN_COLS = sum(IN_SPLITS)
Q_BLOCK = 128
D_FF = 2 * D_MODEL
N_EXPERTS = 8
TOP_K = 2
D_EXPERT = D_MODEL // 2
N_DENSE = (DEPTH + 1) // 2
N_MOE = DEPTH // 2
EPS = 1e-6

kernel_name = "hybrid_diffattn_conformer_shortconv_moe"


def rms_norm(x, g):
    xf = x.astype(jnp.float32)
    y = xf * lax.rsqrt(jnp.mean(xf * xf, axis=-1, keepdims=True) + EPS)
    return (y * g.astype(jnp.float32)).astype(x.dtype)


def layer_norm(x, g, b):
    xf = x.astype(jnp.float32)
    mu = jnp.mean(xf, axis=-1, keepdims=True)
    var = jnp.mean(jnp.square(xf - mu), axis=-1, keepdims=True)
    y = (xf - mu) * lax.rsqrt(var + EPS)
    return (y * g.astype(jnp.float32) + b.astype(jnp.float32)).astype(x.dtype)


def causal_depthwise_conv(u, w):
    k = w.shape[0]
    return lax.conv_general_dilated(
        u, w[:, None, :].astype(u.dtype), window_strides=(1,), padding=[(k - 1, 0)],
        dimension_numbers=("NWC", "WIO", "NWC"), feature_group_count=u.shape[-1])


def diff_attention(q, k, v, q_g, k_g, lq1, lk1, lq2, lk2, subln_g, lam_init):
    b, s, h, _, dh = q.shape
    q = rms_norm(q, q_g) * (dh ** -0.5)
    k = rms_norm(k, k_g)
    lam = (jnp.exp(jnp.sum(lq1.astype(jnp.float32) * lk1.astype(jnp.float32)))
           - jnp.exp(jnp.sum(lq2.astype(jnp.float32) * lk2.astype(jnp.float32)))
           + lam_init)
    qh = q.transpose(0, 2, 3, 1, 4)
    kh = k.transpose(0, 2, 3, 1, 4)
    vh = v.transpose(0, 2, 1, 3)
    nb = s // Q_BLOCK
    qb = jnp.moveaxis(qh.reshape(b, h, 2, nb, Q_BLOCK, dh), 3, 0)
    key_pos = jnp.arange(s)

    def one_block(args):
        q_blk, start = args
        scores = jnp.einsum("bhcqd,bhckd->bhcqk", q_blk, kh).astype(jnp.float32)
        q_pos = start + jnp.arange(Q_BLOCK)
        mask = key_pos[None, :] <= q_pos[:, None]
        p = jax.nn.softmax(jnp.where(mask, scores, -jnp.inf), axis=-1)
        a = p[:, :, 0] - lam * p[:, :, 1]
        return jnp.einsum("bhqk,bhkd->bhqd", a.astype(vh.dtype), vh)

    o = lax.map(one_block, (qb, jnp.arange(nb) * Q_BLOCK))
    o = jnp.moveaxis(o, 0, 2).reshape(b, h, s, 2 * dh)
    o = rms_norm(o, subln_g) * (1.0 - lam_init)
    return o.transpose(0, 2, 1, 3).reshape(b, s, h * 2 * dh)


def swiglu(h, wg, wu, wd):
    return (jax.nn.silu(h @ wg) * (h @ wu)) @ wd


def moe_swiglu(h, router_w, wg, wu, wd):
    logits = (h @ router_w).astype(jnp.float32)
    top_vals, top_idx = lax.top_k(logits, TOP_K)
    gates = jax.nn.softmax(top_vals, axis=-1)
    combine = jnp.einsum("bsk,bske->bse", gates,
                         jax.nn.one_hot(top_idx, N_EXPERTS, dtype=jnp.float32))
    combine = combine.astype(h.dtype)
    y = jnp.zeros_like(h)
    for e in range(N_EXPERTS):
        y = y + combine[..., e:e + 1] * swiglu(h, wg[e], wu[e], wd[e])
    return y


def setup_inputs(seed: int = 0) -> dict:
    key = jax.random.key(seed)
    ks = jax.random.split(key, 24)
    f32 = jnp.float32
    nrm = lambda k, shape, scale: jax.random.normal(k, shape, f32) * scale
    resid = (2.0 * DEPTH) ** -0.5
    return {
        "x": nrm(ks[0], (BATCH, SEQ, D_MODEL), 1.0),
        "norm1_g": 1.0 + nrm(ks[1], (DEPTH, D_MODEL), 0.05),
        "w_in": nrm(ks[2], (DEPTH, D_MODEL, IN_COLS), D_MODEL ** -0.5),
        "q_norm_g": 1.0 + nrm(ks[3], (DEPTH, ATT_HEAD_DIM), 0.05),
        "k_norm_g": 1.0 + nrm(ks[4], (DEPTH, ATT_HEAD_DIM), 0.05),
        "lam_q1": nrm(ks[5], (DEPTH, ATT_HEAD_DIM), 0.1),
        "lam_k1": nrm(ks[6], (DEPTH, ATT_HEAD_DIM), 0.1),
        "lam_q2": nrm(ks[7], (DEPTH, ATT_HEAD_DIM), 0.1),
        "lam_k2": nrm(ks[8], (DEPTH, ATT_HEAD_DIM), 0.1),
        "subln_g": 1.0 + nrm(ks[9], (DEPTH, 2 * ATT_HEAD_DIM), 0.05),
        "conf_dw_w": nrm(ks[10], (DEPTH, CONF_KERNEL, CONF_WIDTH), CONF_KERNEL ** -0.5),
        "conf_dw_b": nrm(ks[11], (DEPTH, CONF_WIDTH), 0.01),
        "conf_ln_g": 1.0 + nrm(ks[12], (DEPTH, CONF_WIDTH), 0.05),
        "conf_ln_b": nrm(ks[13], (DEPTH, CONF_WIDTH), 0.01),
        "sc_w": nrm(ks[14], (DEPTH, SC_KERNEL, SC_WIDTH), SC_KERNEL ** -0.5),
        "w_out": nrm(ks[15], (DEPTH, MIX_WIDTH, D_MODEL), MIX_WIDTH ** -0.5 * resid),
        "norm2_g": 1.0 + nrm(ks[16], (DEPTH, D_MODEL), 0.05),
        "ffn_wg": nrm(ks[17], (N_DENSE, D_MODEL, D_FF), D_MODEL ** -0.5),
        "ffn_wu": nrm(ks[18], (N_DENSE, D_MODEL, D_FF), D_MODEL ** -0.5),
        "ffn_wd": nrm(ks[19], (N_DENSE, D_FF, D_MODEL), D_FF ** -0.5 * resid),
        "router_w": nrm(ks[20], (N_MOE, D_MODEL, N_EXPERTS), D_MODEL ** -0.5),
        "moe_wg": nrm(ks[21], (N_MOE, N_EXPERTS, D_MODEL, D_EXPERT), D_MODEL ** -0.5),
        "moe_wu": nrm(ks[22], (N_MOE, N_EXPERTS, D_MODEL, D_EXPERT), D_MODEL ** -0.5),
        "moe_wd": nrm(ks[23], (N_MOE, N_EXPERTS, D_EXPERT, D_MODEL), D_EXPERT ** -0.5 * resid),
    }


def reference(x, norm1_g, w_in, q_norm_g, k_norm_g, lam_q1, lam_k1, lam_q2, lam_k2,
              subln_g, conf_dw_w, conf_dw_b, conf_ln_g, conf_ln_b, sc_w, w_out,
              norm2_g, ffn_wg, ffn_wu, ffn_wd, router_w, moe_wg, moe_wu, moe_wd):
    b, s, _ = x.shape
    split_at = np.cumsum(IN_SPLITS)[:-1].tolist()
    for l in range(DEPTH):
        h = rms_norm(x, norm1_g[l])
        proj = h @ w_in[l]
        q, k, v, glu_a, glu_b, sc_b, sc_c, sc_h = jnp.split(proj, split_at, axis=-1)
        lam_init = 0.8 - 0.6 * math.exp(-0.3 * l)
        att = diff_attention(
            q.reshape(b, s, ATT_HEADS, 2, ATT_HEAD_DIM),
            k.reshape(b, s, ATT_HEADS, 2, ATT_HEAD_DIM),
            v.reshape(b, s, ATT_HEADS, 2 * ATT_HEAD_DIM),
            q_norm_g[l], k_norm_g[l], lam_q1[l], lam_k1[l], lam_q2[l], lam_k2[l],
            subln_g[l], lam_init)
        u = glu_a * jax.nn.sigmoid(glu_b)
        u = causal_depthwise_conv(u, conf_dw_w[l]) + conf_dw_b[l]
        conf = jax.nn.silu(layer_norm(u, conf_ln_g[l], conf_ln_b[l]))
        sc = sc_b * causal_depthwise_conv(sc_c * sc_h, sc_w[l])
        x = x + jnp.concatenate([att, conf, sc], axis=-1) @ w_out[l]
        h = rms_norm(x, norm2_g[l])
        if l % 2 == 0:
            i = l // 2
            x = x + swiglu(h, ffn_wg[i], ffn_wu[i], ffn_wd[i])
        else:
            i = l // 2
            x = x + moe_swiglu(h, router_w[i], moe_wg[i], moe_wu[i], moe_wd[i])
    return x
```

```python
import functools
import math

import jax
import jax.numpy as jnp
from jax import lax
from jax.experimental import pallas as pl
from jax.experimental.pallas import tpu as pltpu

EPS = 1e-6
BF16 = jnp.bfloat16
F32 = jnp.float32

LANES = 128
MXU_DIM = 256
VMEM_LIMIT = 56 * 1024 * 1024
ATT_HEAD_DIM = 128
CONF_KERNEL = 31
SC_KERNEL = 3
TOP_K = 2
CONV_HALO = 32
CONV_ROWS = 16


def _params(*sem):
    return pltpu.CompilerParams(dimension_semantics=sem, vmem_limit_bytes=VMEM_LIMIT)


def _tile(n, pref):
    t = min(n, pref)
    while n % t:
        t -= 8
    return t


def _rmsnorm_body(x_ref, g_ref, h_ref):
    x = x_ref[...]
    ms = jnp.mean(x * x, axis=-1, keepdims=True)
    h_ref[...] = (x * lax.rsqrt(ms + EPS) * g_ref[...]).astype(h_ref.dtype)


def _rmsnorm(x, g, tr=256):
    s, d = x.shape
    tr = _tile(s, tr)
    return pl.pallas_call(
        _rmsnorm_body,
        grid=(s // tr,),
        in_specs=[pl.BlockSpec((tr, d), lambda i: (i, 0)), pl.BlockSpec((1, d), lambda i: (0, 0))],
        out_specs=pl.BlockSpec((tr, d), lambda i: (i, 0)),
        out_shape=jax.ShapeDtypeStruct((s, d), BF16),
        compiler_params=_params("parallel"),
        name="rmsnorm",
    )(x, g.reshape(1, d))


def _rmsnorm_router_body(x_ref, g_ref, rw_ref, h_ref, c_ref, *, n_experts):
    x = x_ref[...]
    ms = jnp.mean(x * x, axis=-1, keepdims=True)
    y = x * lax.rsqrt(ms + EPS) * g_ref[...]
    h_ref[...] = y.astype(h_ref.dtype)
    logits = jnp.dot(y, rw_ref[...], preferred_element_type=F32, precision=lax.Precision.HIGHEST)
    lane = lax.broadcasted_iota(jnp.int32, logits.shape, 1).astype(F32)
    logits = jnp.where(lane < n_experts, logits, -jnp.inf)
    t1 = jnp.max(logits, axis=-1, keepdims=True)
    i1 = jnp.min(jnp.where(logits == t1, lane, float(LANES)), axis=-1, keepdims=True)
    rest = jnp.where(lane == i1, -jnp.inf, logits)
    t2 = jnp.max(rest, axis=-1, keepdims=True)
    i2 = jnp.min(jnp.where(rest == t2, lane, float(LANES)), axis=-1, keepdims=True)
    e2 = jnp.exp(t2 - t1)
    den = 1.0 + e2
    c_ref[...] = jnp.where(lane == i1, 1.0 / den, 0.0) + jnp.where(lane == i2, e2 / den, 0.0)


def _rmsnorm_router(x, g, router_w, tr=256):
    s, d = x.shape
    n_experts = router_w.shape[1]
    tr = _tile(s, tr)
    rw = jnp.pad(router_w, ((0, 0), (0, LANES - n_experts)))
    return pl.pallas_call(
        functools.partial(_rmsnorm_router_body, n_experts=n_experts),
        grid=(s // tr,),
        in_specs=[pl.BlockSpec((tr, d), lambda i: (i, 0)), pl.BlockSpec((1, d), lambda i: (0, 0)),
                  pl.BlockSpec((d, LANES), lambda i: (0, 0))],
        out_specs=[pl.BlockSpec((tr, d), lambda i: (i, 0)), pl.BlockSpec((tr, LANES), lambda i: (i, 0))],
        out_shape=[jax.ShapeDtypeStruct((s, d), BF16), jax.ShapeDtypeStruct((s, LANES), F32)],
        compiler_params=_params("parallel"),
        name="rmsnorm_router",
    )(x, g.reshape(1, d), rw)


def _mm_in_body(a_ref, b_ref, gain_ref, scale_ref, o_ref, *, n_norm_tiles):
    acc = jnp.dot(a_ref[...], b_ref[...], preferred_element_type=F32)
    j = pl.program_id(1)

    @pl.when(j >= n_norm_tiles)
    def _():
        o_ref[...] = acc.astype(o_ref.dtype)

    @pl.when(j < n_norm_tiles)
    def _():
        for g in range(acc.shape[1] // ATT_HEAD_DIM):
            sl = slice(g * ATT_HEAD_DIM, (g + 1) * ATT_HEAD_DIM)
            blk = acc[:, sl]
            ms = jnp.mean(blk * blk, axis=-1, keepdims=True)
            y = blk * lax.rsqrt(ms + EPS) * gain_ref[:, sl] * scale_ref[:, sl]
            o_ref[:, sl] = y.astype(o_ref.dtype)


def _mm_in(h, w, gain, scale, n_norm_cols, tm=1024):
    m, k = h.shape
    n = w.shape[1]
    tn = k // 4
    tm = _tile(m, tm)
    assert n % tn == 0 and n_norm_cols % tn == 0 and tn % ATT_HEAD_DIM == 0
    return pl.pallas_call(
        functools.partial(_mm_in_body, n_norm_tiles=n_norm_cols // tn),
        grid=(m // tm, n // tn),
        in_specs=[pl.BlockSpec((tm, k), lambda i, j: (i, 0)), pl.BlockSpec((k, tn), lambda i, j: (0, j)),
                  pl.BlockSpec((1, tn), lambda i, j: (0, j)), pl.BlockSpec((1, tn), lambda i, j: (0, j))],
        out_specs=pl.BlockSpec((tm, tn), lambda i, j: (i, j)),
        out_shape=jax.ShapeDtypeStruct((m, n), BF16),
        compiler_params=_params("parallel", "arbitrary"),
        name="in_proj",
    )(h, w, gain, scale)


def _mm_resid_body(a_ref, b_ref, r_ref, o_ref):
    d = jnp.dot(a_ref[...], b_ref[...], preferred_element_type=F32)
    kk = pl.program_id(2)

    @pl.when(kk == 0)
    def _():
        o_ref[...] = r_ref[...] + d

    @pl.when(kk > 0)
    def _():
        o_ref[...] += d


def _mm_resid(a, w, r, tm=1024, tn=512, tk=4096):
    m, k = a.shape
    n = w.shape[1]
    tm, tn, tk = _tile(m, tm), _tile(n, tn), _tile(k, tk)
    return pl.pallas_call(
        _mm_resid_body,
        grid=(m // tm, n // tn, k // tk),
        in_specs=[pl.BlockSpec((tm, tk), lambda i, j, kk: (i, kk)), pl.BlockSpec((tk, tn), lambda i, j, kk: (kk, j)),
                  pl.BlockSpec((tm, tn), lambda i, j, kk: (i, j))],
        out_specs=pl.BlockSpec((tm, tn), lambda i, j, kk: (i, j)),
        out_shape=jax.ShapeDtypeStruct((m, n), F32),
        compiler_params=_params("parallel", "parallel", "arbitrary"),
        name="mm_resid",
    )(a, w, r)


def _mm_gated_body(a_ref, wg_ref, wu_ref, o_ref):
    a = a_ref[...]
    g = jnp.dot(a, wg_ref[...], preferred_element_type=F32)
    u = jnp.dot(a, wu_ref[...], preferred_element_type=F32)
    o_ref[...] = (g * jax.nn.sigmoid(g) * u).astype(o_ref.dtype)


def _mm_gated(h, wg, wu, tm=1024, tn=512):
    m, k = h.shape
    n = wg.shape[1]
    tm, tn = _tile(m, tm), _tile(n, tn)
    return pl.pallas_call(
        _mm_gated_body,
        grid=(m // tm, n // tn),
        in_specs=[pl.BlockSpec((tm, k), lambda i, j: (i, 0)), pl.BlockSpec((k, tn), lambda i, j: (0, j)),
                  pl.BlockSpec((k, tn), lambda i, j: (0, j))],
        out_specs=pl.BlockSpec((tm, tn), lambda i, j: (i, j)),
        out_shape=jax.ShapeDtypeStruct((m, n), BF16),
        compiler_params=_params("parallel", "arbitrary"),
        name="mm_gated",
    )(h, wg, wu)


def _mm_gated_moe_body(a_ref, c_ref, wg_ref, wu_ref, o_ref):
    a = a_ref[...]
    e = pl.program_id(1)
    g = jnp.dot(a, wg_ref[...], preferred_element_type=F32)
    u = jnp.dot(a, wu_ref[...], preferred_element_type=F32)
    c = c_ref[...]
    lane = lax.broadcasted_iota(jnp.int32, c.shape, 1)
    ce = jnp.sum(jnp.where(lane == e, c, 0.0), axis=-1, keepdims=True)
    o_ref[...] = (ce * (g * jax.nn.sigmoid(g) * u)).astype(o_ref.dtype)


def _mm_gated_moe(h, combine, wg, wu, tm=1024, tn=512):
    m, k = h.shape
    n_e, _, f = wg.shape
    tm, tn = _tile(m, tm), _tile(f, tn)
    nf = f // tn
    return pl.pallas_call(
        _mm_gated_moe_body,
        grid=(m // tm, n_e, nf),
        in_specs=[pl.BlockSpec((tm, k), lambda i, e, j: (i, 0)), pl.BlockSpec((tm, LANES), lambda i, e, j: (i, 0)),
                  pl.BlockSpec((None, k, tn), lambda i, e, j: (e, 0, j)),
                  pl.BlockSpec((None, k, tn), lambda i, e, j: (e, 0, j))],
        out_specs=pl.BlockSpec((tm, tn), lambda i, e, j: (i, e * nf + j)),
        out_shape=jax.ShapeDtypeStruct((m, n_e * f), BF16),
        compiler_params=_params("parallel", "arbitrary", "arbitrary"),
        name="mm_gated_moe",
    )(h, combine, wg, wu)


def _attn_body(lq1_ref, lk1_ref, lq2_ref, lk2_ref, sg_ref, q_ref, k_ref, v_ref, o_ref, *, tq, lam_init):
    dh = ATT_HEAD_DIM
    i = pl.program_id(1)
    lam = (jnp.exp(jnp.sum(lq1_ref[...] * lk1_ref[...], axis=-1, keepdims=True))
           - jnp.exp(jnp.sum(lq2_ref[...] * lk2_ref[...], axis=-1, keepdims=True)) + lam_init)
    q = q_ref[...]
    qs = (q[:, :dh], q[:, dh:])
    nt = (((1,), (1,)), ((), ()))

    def chunk(j, carry, masked):
        start = pl.multiple_of(j * tq, tq)
        kc = k_ref[pl.ds(start, tq), :]
        vc = v_ref[pl.ds(start, tq), :]
        if masked:
            row = lax.broadcasted_iota(jnp.int32, (tq, tq), 0)
            col = lax.broadcasted_iota(jnp.int32, (tq, tq), 1)
            keep = col <= row
        out = []
        for c in range(2):
            m_old, l_old, acc_old = carry[c]
            s = lax.dot_general(qs[c], kc[:, c * dh:(c + 1) * dh], nt, preferred_element_type=F32)
            if masked:
                s = jnp.where(keep, s, -jnp.inf)
            m_new = jnp.maximum(m_old, jnp.max(s, axis=-1, keepdims=True))
            alpha = jnp.exp(m_old - m_new)
            p = jnp.exp(s - m_new)
            l_new = alpha * l_old + jnp.sum(p, axis=-1, keepdims=True)
            acc_new = alpha * acc_old + jnp.dot(p.astype(vc.dtype), vc, preferred_element_type=F32)
            out.append((m_new, l_new, acc_new))
        return tuple(out)

    init1 = (jnp.full((tq, 1), -jnp.inf, F32), jnp.zeros((tq, 1), F32), jnp.zeros((tq, 2 * dh), F32))
    carry = lax.fori_loop(0, i, lambda j, c: chunk(j, c, False), (init1, init1))
    (_, l1, a1), (_, l2, a2) = chunk(i, carry, True)
    o = a1 / l1 - lam * (a2 / l2)
    ms = jnp.mean(o * o, axis=-1, keepdims=True)
    y = o * lax.rsqrt(ms + EPS) * sg_ref[...] * (1.0 - lam_init)
    o_ref[...] = y.astype(o_ref.dtype)


def _diff_attention(proj, lq1, lk1, lq2, lk2, subln_g, n_heads, lam_init, tq=512):
    s = proj.shape[0]
    dh = ATT_HEAD_DIM
    hw = 2 * dh
    tq = _tile(s, tq)
    vec = lambda h, i: (0, 0)
    return pl.pallas_call(
        functools.partial(_attn_body, tq=tq, lam_init=lam_init),
        grid=(n_heads, s // tq),
        in_specs=[pl.BlockSpec((1, dh), vec), pl.BlockSpec((1, dh), vec), pl.BlockSpec((1, dh), vec),
                  pl.BlockSpec((1, dh), vec), pl.BlockSpec((1, hw), vec),
                  pl.BlockSpec((tq, hw), lambda h, i: (i, h)),
                  pl.BlockSpec((s, hw), lambda h, i: (0, n_heads + h)),
                  pl.BlockSpec((s, hw), lambda h, i: (0, 2 * n_heads + h))],
        out_specs=pl.BlockSpec((tq, hw), lambda h, i: (i, h)),
        out_shape=jax.ShapeDtypeStruct((s, n_heads * hw), BF16),
        compiler_params=_params("parallel", "arbitrary"),
        name="diff_attention",
    )(lq1.reshape(1, dh), lk1.reshape(1, dh), lq2.reshape(1, dh), lk2.reshape(1, dh), subln_g.reshape(1, hw),
      proj, proj, proj)


def _causal_dwconv(w_ref, src_ref, n_taps, ts, width, finish):
    for r0 in range(0, ts, CONV_ROWS):
        acc = jnp.zeros((CONV_ROWS, width), F32)
        for j in range(n_taps):
            start = r0 + CONV_HALO - (n_taps - 1) + j
            acc = acc + w_ref[j:j + 1, :] * src_ref[start:start + CONV_ROWS, :]
        finish(r0, acc)


def _conv_body(ga_ref, gb_ref, sb_ref, scc_ref, sh_ref, ga_h, gb_h, scc_h, sh_h,
               dw_ref, db_ref, lng_ref, lnb_ref, scw_ref, conf_ref, sc_ref, u_scr, p_scr, *, ts):
    first = pl.program_id(0) == 0
    width = u_scr.shape[1]
    f32 = lambda r: r[...].astype(F32)
    u_scr[0:CONV_HALO, :] = jnp.where(first, 0.0, f32(ga_h) * jax.nn.sigmoid(f32(gb_h)))
    u_scr[CONV_HALO:CONV_HALO + ts, :] = f32(ga_ref) * jax.nn.sigmoid(f32(gb_ref))
    p_scr[0:CONV_HALO, :] = jnp.where(first, 0.0, f32(scc_h) * f32(sh_h))
    p_scr[CONV_HALO:CONV_HALO + ts, :] = f32(scc_ref) * f32(sh_ref)

    def finish_conf(r0, acc):
        u = acc + db_ref[...]
        mu = jnp.mean(u, axis=-1, keepdims=True)
        var = jnp.mean(jnp.square(u - mu), axis=-1, keepdims=True)
        y = (u - mu) * lax.rsqrt(var + EPS) * lng_ref[...] + lnb_ref[...]
        conf_ref[pl.ds(r0, CONV_ROWS), :] = (y * jax.nn.sigmoid(y)).astype(conf_ref.dtype)

    def finish_sc(r0, acc):
        sc_ref[pl.ds(r0, CONV_ROWS), :] = (sb_ref[pl.ds(r0, CONV_ROWS), :].astype(F32) * acc).astype(sc_ref.dtype)

    _causal_dwconv(dw_ref, u_scr, CONF_KERNEL, ts, width, finish_conf)
    _causal_dwconv(scw_ref, p_scr, SC_KERNEL, ts, width, finish_sc)


def _conv_modules(proj, dw_w, dw_b, ln_g, ln_b, sc_w, width, col0, ts=256):
    s = proj.shape[0]
    ts = _tile(s, ts)
    assert ts % CONV_HALO == 0 and CONV_HALO >= CONF_KERNEL - 1
    hb = ts // CONV_HALO
    cur = lambda c: pl.BlockSpec((ts, width), lambda i: (i, col0 + c))
    halo = lambda c: pl.BlockSpec((CONV_HALO, width), lambda i: (jnp.maximum(i * hb - 1, 0), col0 + c))
    par = lambda n: pl.BlockSpec((n, width), lambda i: (0, 0))
    out = pl.BlockSpec((ts, width), lambda i: (i, 0))
    return pl.pallas_call(
        functools.partial(_conv_body, ts=ts),
        grid=(s // ts,),
        in_specs=[cur(0), cur(1), cur(2), cur(3), cur(4), halo(0), halo(1), halo(3), halo(4),
                  par(CONF_KERNEL), par(1), par(1), par(1), par(SC_KERNEL)],
        out_specs=[out, out],
        out_shape=[jax.ShapeDtypeStruct((s, width), BF16), jax.ShapeDtypeStruct((s, width), BF16)],
        scratch_shapes=[pltpu.VMEM((CONV_HALO + ts, width), F32), pltpu.VMEM((CONV_HALO + ts, width), F32)],
        compiler_params=_params("parallel"),
        name="conv_modules",
    )(proj, proj, proj, proj, proj, proj, proj, proj, proj,
      dw_w, dw_b.reshape(1, width), ln_g.reshape(1, width), ln_b.reshape(1, width), sc_w)


def kernel(x, norm1_g, w_in, q_norm_g, k_norm_g, lam_q1, lam_k1, lam_q2, lam_k2, subln_g, conf_dw_w, conf_dw_b, conf_ln_g, conf_ln_b, sc_w, w_out, norm2_g, ffn_wg, ffn_wu, ffn_wd, router_w, moe_wg, moe_wu, moe_wd):
    b, s, d = x.shape
    depth = w_in.shape[0]
    dh = ATT_HEAD_DIM
    att_width = d // 2
    n_heads = att_width // (2 * dh)
    width = d // 4
    in_cols = w_in.shape[2]
    n_experts = router_w.shape[2]

    bf = lambda w: w.astype(BF16)
    w_in_b, w_out_b = bf(w_in), bf(w_out)
    ffn_wg_b, ffn_wu_b, ffn_wd_b = bf(ffn_wg), bf(ffn_wu), bf(ffn_wd)
    moe_wg_b, moe_wu_b, moe_wd_b = bf(moe_wg), bf(moe_wu), bf(moe_wd)

    ones_rest = jnp.ones((in_cols - 2 * att_width,), F32)
    qk_scale = jnp.concatenate([jnp.full((att_width,), dh ** -0.5, F32), jnp.ones((in_cols - att_width,), F32)])

    outs = []
    for bi in range(b):
        xs = x[bi]
        for l in range(depth):
            lam_init = 0.8 - 0.6 * math.exp(-0.3 * l)
            h = _rmsnorm(xs, norm1_g[l])
            gain = jnp.concatenate([jnp.tile(q_norm_g[l], att_width // dh), jnp.tile(k_norm_g[l], att_width // dh), ones_rest])
            proj = _mm_in(h, w_in_b[l], gain.reshape(1, in_cols), qk_scale.reshape(1, in_cols), 2 * att_width)
            att = _diff_attention(proj, lam_q1[l], lam_k1[l], lam_q2[l], lam_k2[l], subln_g[l], n_heads, lam_init)
            conf, sc = _conv_modules(proj, conf_dw_w[l], conf_dw_b[l], conf_ln_g[l], conf_ln_b[l], sc_w[l],
                                     width, 3 * att_width // width)
            mix = jnp.concatenate([att, conf, sc], axis=-1)
            xs = _mm_resid(mix, w_out_b[l], xs)
            i = l // 2
            if l % 2 == 0:
                h = _rmsnorm(xs, norm2_g[l])
                hid = _mm_gated(h, ffn_wg_b[i], ffn_wu_b[i])
                xs = _mm_resid(hid, ffn_wd_b[i], xs)
            else:
                h, combine = _rmsnorm_router(xs, norm2_g[l], router_w[i])
                hid = _mm_gated_moe(h, combine, moe_wg_b[i], moe_wu_b[i])
                xs = _mm_resid(hid, moe_wd_b[i].reshape(n_experts * moe_wd.shape[2], d), xs)
        outs.append(xs)
    return jnp.stack(outs)
```

```python
import functools
import math

import jax
import jax.numpy as jnp
from jax import lax
from jax.experimental import pallas as pl
from jax.experimental.pallas import tpu as pltpu

EPS = 1e-6
BF16 = jnp.bfloat16
F32 = jnp.float32
I32 = jnp.int32

LANES = 128
VMEM_LIMIT = 56 * 1024 * 1024
ATT_HEAD_DIM = 128
CONF_KERNEL = 31
SC_KERNEL = 3
CONV_HALO = 32
CONV_ROWS = 16
ATT_STRIP = 32
MOE_TILE = 512
LOG2E = math.log2(math.e)


def _params(*sem):
    return pltpu.CompilerParams(dimension_semantics=sem, vmem_limit_bytes=VMEM_LIMIT)


def _tile(n, pref):
    t = min(n, pref)
    while n % t:
        t -= 8
    return t


def _rmsnorm_body(x_ref, g_ref, h_ref):
    x = x_ref[...]
    ms = jnp.mean(x * x, axis=-1, keepdims=True)
    h_ref[...] = (x * lax.rsqrt(ms + EPS) * g_ref[...]).astype(h_ref.dtype)


def _rmsnorm(x, g, tr=256):
    s, d = x.shape
    tr = _tile(s, tr)
    return pl.pallas_call(
        _rmsnorm_body,
        grid=(s // tr,),
        in_specs=[pl.BlockSpec((tr, d), lambda i: (i, 0)), pl.BlockSpec((1, d), lambda i: (0, 0))],
        out_specs=pl.BlockSpec((tr, d), lambda i: (i, 0)),
        out_shape=jax.ShapeDtypeStruct((s, d), BF16),
        compiler_params=_params("parallel"),
        name="rmsnorm",
    )(x, g.reshape(1, d))


def _rmsnorm_router_body(x_ref, g_ref, rw_ref, h_ref, r_ref, *, n_experts):
    x = x_ref[...]
    ms = jnp.mean(x * x, axis=-1, keepdims=True)
    y = x * lax.rsqrt(ms + EPS) * g_ref[...]
    h_ref[...] = y.astype(h_ref.dtype).reshape(h_ref.shape)
    logits = jnp.dot(y, rw_ref[...], preferred_element_type=F32, precision=lax.Precision.HIGHEST)
    lane = lax.broadcasted_iota(I32, logits.shape, 1).astype(F32)
    logits = jnp.where(lane < n_experts, logits, -jnp.inf)
    t1 = jnp.max(logits, axis=-1, keepdims=True)
    i1 = jnp.min(jnp.where(logits == t1, lane, float(LANES)), axis=-1, keepdims=True)
    rest = jnp.where(lane == i1, -jnp.inf, logits)
    t2 = jnp.max(rest, axis=-1, keepdims=True)
    i2 = jnp.min(jnp.where(rest == t2, lane, float(LANES)), axis=-1, keepdims=True)
    e2 = jnp.exp(t2 - t1)
    den = 1.0 + e2
    r_ref[...] = (jnp.where(lane == 0.0, i1, 0.0) + jnp.where(lane == 1.0, i2, 0.0)
                  + jnp.where(lane == 2.0, 1.0 / den, 0.0) + jnp.where(lane == 3.0, e2 / den, 0.0))


def _rmsnorm_router(x, g, router_w, tr=256):
    s, d = x.shape
    n_experts = router_w.shape[1]
    tr = _tile(s, tr)
    rw = jnp.pad(router_w, ((0, 0), (0, LANES - n_experts)))
    return pl.pallas_call(
        functools.partial(_rmsnorm_router_body, n_experts=n_experts),
        grid=(s // tr,),
        in_specs=[pl.BlockSpec((tr, d), lambda i: (i, 0)), pl.BlockSpec((1, d), lambda i: (0, 0)),
                  pl.BlockSpec((d, LANES), lambda i: (0, 0))],
        out_specs=[pl.BlockSpec((tr, d // LANES, LANES), lambda i: (i, 0, 0)), pl.BlockSpec((tr, LANES), lambda i: (i, 0))],
        out_shape=[jax.ShapeDtypeStruct((s, d // LANES, LANES), BF16), jax.ShapeDtypeStruct((s, LANES), F32)],
        compiler_params=_params("parallel"),
        name="rmsnorm_router",
    )(x, g.reshape(1, d), rw)


def _mm_in_body(a_ref, b_ref, gain_ref, scale_ref, o_ref, *, n_norm_tiles):
    acc = jnp.dot(a_ref[...], b_ref[...], preferred_element_type=F32)
    j = pl.program_id(1)

    @pl.when(j >= n_norm_tiles)
    def _():
        o_ref[...] = acc.astype(o_ref.dtype)

    @pl.when(j < n_norm_tiles)
    def _():
        for g in range(acc.shape[1] // ATT_HEAD_DIM):
            sl = slice(g * ATT_HEAD_DIM, (g + 1) * ATT_HEAD_DIM)
            blk = acc[:, sl]
            ms = jnp.mean(blk * blk, axis=-1, keepdims=True)
            y = blk * lax.rsqrt(ms + EPS) * gain_ref[:, sl] * scale_ref[:, sl]
            o_ref[:, sl] = y.astype(o_ref.dtype)


def _mm_in(h, w, gain, scale, n_norm_cols, tm=1024):
    m, k = h.shape
    n = w.shape[1]
    tn = k // 4
    tm = _tile(m, tm)
    assert n % tn == 0 and n_norm_cols % tn == 0 and tn % ATT_HEAD_DIM == 0
    return pl.pallas_call(
        functools.partial(_mm_in_body, n_norm_tiles=n_norm_cols // tn),
        grid=(m // tm, n // tn),
        in_specs=[pl.BlockSpec((tm, k), lambda i, j: (i, 0)), pl.BlockSpec((k, tn), lambda i, j: (0, j)),
                  pl.BlockSpec((1, tn), lambda i, j: (0, j)), pl.BlockSpec((1, tn), lambda i, j: (0, j))],
        out_specs=pl.BlockSpec((tm, tn), lambda i, j: (i, j)),
        out_shape=jax.ShapeDtypeStruct((m, n), BF16),
        compiler_params=_params("parallel", "arbitrary"),
        name="in_proj",
    )(h, w, gain, scale)


def _mm_resid_body(a_ref, b_ref, r_ref, o_ref):
    d = jnp.dot(a_ref[...], b_ref[...], preferred_element_type=F32)
    kk = pl.program_id(2)

    @pl.when(kk == 0)
    def _():
        o_ref[...] = r_ref[...] + d

    @pl.when(kk > 0)
    def _():
        o_ref[...] += d


def _mm_resid(a, w, r, tm=1024, tn=512, tk=4096):
    m, k = a.shape
    n = w.shape[1]
    tm, tn, tk = _tile(m, tm), _tile(n, tn), _tile(k, tk)
    return pl.pallas_call(
        _mm_resid_body,
        grid=(m // tm, n // tn, k // tk),
        in_specs=[pl.BlockSpec((tm, tk), lambda i, j, kk: (i, kk)), pl.BlockSpec((tk, tn), lambda i, j, kk: (kk, j)),
                  pl.BlockSpec((tm, tn), lambda i, j, kk: (i, j))],
        out_specs=pl.BlockSpec((tm, tn), lambda i, j, kk: (i, j)),
        out_shape=jax.ShapeDtypeStruct((m, n), F32),
        compiler_params=_params("parallel", "parallel", "arbitrary"),
        name="mm_resid",
    )(a, w, r)


def _mm_gated_body(a_ref, wg_ref, wu_ref, o_ref):
    a = a_ref[...]
    g = jnp.dot(a, wg_ref[...], preferred_element_type=F32)
    u = jnp.dot(a, wu_ref[...], preferred_element_type=F32)
    o_ref[...] = (g * jax.nn.sigmoid(g) * u).astype(o_ref.dtype)


def _mm_gated(h, wg, wu, tm=1024, tn=512):
    m, k = h.shape
    n = wg.shape[1]
    tm, tn = _tile(m, tm), _tile(n, tn)
    return pl.pallas_call(
        _mm_gated_body,
        grid=(m // tm, n // tn),
        in_specs=[pl.BlockSpec((tm, k), lambda i, j: (i, 0)), pl.BlockSpec((k, tn), lambda i, j: (0, j)),
                  pl.BlockSpec((k, tn), lambda i, j: (0, j))],
        out_specs=pl.BlockSpec((tm, tn), lambda i, j: (i, j)),
        out_shape=jax.ShapeDtypeStruct((m, n), BF16),
        compiler_params=_params("parallel", "arbitrary"),
        name="mm_gated",
    )(h, wg, wu)


def _route(r_tab, n_experts, tm):
    s = r_tab.shape[0]
    e_flat = jnp.concatenate([r_tab[:, 0], r_tab[:, 1]]).astype(I32)
    onehot = (e_flat[:, None] == jnp.arange(n_experts, dtype=I32)[None, :]).astype(I32)
    csum = jnp.cumsum(onehot, axis=0)
    rank = jnp.sum((csum - onehot) * onehot, axis=1)
    counts = csum[-1]
    padded = ((counts + tm - 1) // tm) * tm
    ends = jnp.cumsum(padded)
    dest = (ends - padded)[e_flat] + rank
    p_rows = 2 * s + n_experts * tm
    src = jnp.zeros((p_rows,), I32).at[dest].set(jnp.arange(2 * s, dtype=I32) % s)
    tile_ends = ends // tm
    tiles = jnp.arange(p_rows // tm, dtype=I32)
    tile_expert = jnp.minimum(jnp.sum((tiles[:, None] >= tile_ends[None, :]).astype(I32), axis=1), n_experts - 1)
    return src, dest.reshape(2, s), tile_expert, tile_ends[-1:].astype(I32)


def _gather_rows(idx_ref, base, n, src_hbm, dst_ref, sem):
    def copy(r):
        return pltpu.make_async_copy(src_hbm.at[idx_ref[base + r]], dst_ref.at[r], sem)

    def start(r, _):
        copy(r).start()
        return 0

    def wait(r, _):
        copy(r).wait()
        return 0

    lax.fori_loop(0, n, start, 0)
    lax.fori_loop(0, n, wait, 0)


def _moe_up_body(src_ref, te_ref, nv_ref, h3_hbm, wg_ref, wu_ref, o_ref, a3_scr, a_scr, sem, *, tm):
    i = pl.program_id(0)
    j = pl.program_id(1)
    live = i < nv_ref[0]

    @pl.when(jnp.logical_and(live, j == 0))
    def _():
        _gather_rows(src_ref, i * tm, tm, h3_hbm, a3_scr, sem)
        a_scr[...] = a3_scr[...].reshape(a_scr.shape)

    @pl.when(live)
    def _():
        a = a_scr[...]
        g = jnp.dot(a, wg_ref[...], preferred_element_type=F32)
        u = jnp.dot(a, wu_ref[...], preferred_element_type=F32)
        o_ref[...] = (g * jax.nn.sigmoid(g) * u).astype(o_ref.dtype)

    @pl.when(jnp.logical_not(live))
    def _():
        o_ref[...] = jnp.zeros(o_ref.shape, o_ref.dtype)


def _moe_up(h3, src, tile_expert, n_live, wg, wu, tm, tn=512):
    s, ns, nl = h3.shape
    d = ns * nl
    f = wg.shape[2]
    p_rows = src.shape[0]
    tn = _tile(f, tn)
    w_spec = pl.BlockSpec((None, d, tn), lambda i, j, src, te, nv: (te[i], 0, j))
    return pl.pallas_call(
        functools.partial(_moe_up_body, tm=tm),
        grid_spec=pltpu.PrefetchScalarGridSpec(
            num_scalar_prefetch=3,
            grid=(p_rows // tm, f // tn),
            in_specs=[pl.BlockSpec(memory_space=pl.ANY), w_spec, w_spec],
            out_specs=pl.BlockSpec((tm, tn), lambda i, j, src, te, nv: (i, j)),
            scratch_shapes=[pltpu.VMEM((tm, ns, nl), BF16), pltpu.VMEM((tm, d), BF16), pltpu.SemaphoreType.DMA(())]),
        out_shape=jax.ShapeDtypeStruct((p_rows, f), BF16),
        compiler_params=_params("arbitrary", "arbitrary"),
        name="moe_up",
    )(src, tile_expert, n_live, h3, wg, wu)


def _moe_down_body(te_ref, nv_ref, a_ref, w_ref, o_ref):
    live = pl.program_id(0) < nv_ref[0]

    @pl.when(live)
    def _():
        o_ref[...] = jnp.dot(a_ref[...], w_ref[...], preferred_element_type=F32).reshape(o_ref.shape)

    @pl.when(jnp.logical_not(live))
    def _():
        o_ref[...] = jnp.zeros(o_ref.shape, o_ref.dtype)


def _moe_down(hid, tile_expert, n_live, wd, tm, tn=1024):
    p_rows, f = hid.shape
    d = wd.shape[2]
    tn = _tile(d, tn)
    assert tn % (8 * LANES) == 0
    return pl.pallas_call(
        _moe_down_body,
        grid_spec=pltpu.PrefetchScalarGridSpec(
            num_scalar_prefetch=2,
            grid=(p_rows // tm, d // tn),
            in_specs=[pl.BlockSpec((tm, f), lambda i, j, te, nv: (i, 0)),
                      pl.BlockSpec((None, f, tn), lambda i, j, te, nv: (te[i], 0, j))],
            out_specs=pl.BlockSpec((tm, tn // LANES, LANES), lambda i, j, te, nv: (i, j, 0))),
        out_shape=jax.ShapeDtypeStruct((p_rows, d // LANES, LANES), F32),
        compiler_params=_params("arbitrary", "arbitrary"),
        name="moe_down",
    )(tile_expert, n_live, hid, wd)


def _moe_combine_body(slot_ref, x_ref, r_ref, g_ref, y3_hbm, xo_ref, *rest, tc, s, norm):
    if norm:
        h_ref, b1, b2, sem = rest
    else:
        b1, b2, sem = rest
    i = pl.program_id(0)
    _gather_rows(slot_ref, i * tc, tc, y3_hbm, b1, sem)
    _gather_rows(slot_ref, s + i * tc, tc, y3_hbm, b2, sem)
    r = r_ref[...]
    xn = x_ref[...] + r[:, 2:3] * b1[...].reshape(x_ref.shape) + r[:, 3:4] * b2[...].reshape(x_ref.shape)
    xo_ref[...] = xn
    if norm:
        ms = jnp.mean(xn * xn, axis=-1, keepdims=True)
        h_ref[...] = (xn * lax.rsqrt(ms + EPS) * g_ref[...]).astype(h_ref.dtype)


def _moe_combine(x, r_tab, slot, y3, next_g, tc=256):
    s, d = x.shape
    tc = _tile(s, tc)
    ns, nl = y3.shape[1:]
    norm = next_g is not None
    g = (next_g if norm else jnp.ones((d,), F32)).reshape(1, d)
    row = pl.BlockSpec((tc, d), lambda i, slot: (i, 0))
    out_specs = [row, row] if norm else [row]
    out_shape = [jax.ShapeDtypeStruct((s, d), F32)] + ([jax.ShapeDtypeStruct((s, d), BF16)] if norm else [])
    res = pl.pallas_call(
        functools.partial(_moe_combine_body, tc=tc, s=s, norm=norm),
        grid_spec=pltpu.PrefetchScalarGridSpec(
            num_scalar_prefetch=1,
            grid=(s // tc,),
            in_specs=[row, pl.BlockSpec((tc, LANES), lambda i, slot: (i, 0)), pl.BlockSpec((1, d), lambda i, slot: (0, 0)),
                      pl.BlockSpec(memory_space=pl.ANY)],
            out_specs=out_specs,
            scratch_shapes=[pltpu.VMEM((tc, ns, nl), F32), pltpu.VMEM((tc, ns, nl), F32), pltpu.SemaphoreType.DMA(())]),
        out_shape=out_shape,
        compiler_params=_params("arbitrary"),
        name="moe_combine",
    )(slot.reshape(2 * s), x, r_tab, g, y3)
    return (res[0], res[1]) if norm else (res[0], None)


def _attn_body(lq1_ref, lk1_ref, lq2_ref, lk2_ref, sg_ref, q_ref, k_ref, v_ref, o_ref,
               s_scr, p_scr, m_scr, l_scr, acc_scr, *, t, lam_init):
    dh = ATT_HEAD_DIM
    rs = ATT_STRIP
    i = pl.program_id(1)
    nt = (((1,), (1,)), ((), ()))

    m_scr[...] = jnp.full(m_scr.shape, -jnp.inf, F32)
    l_scr[...] = jnp.zeros(l_scr.shape, F32)
    acc_scr[...] = jnp.zeros(acc_scr.shape, F32)

    def chunk(j, masked):
        start = pl.multiple_of(j * t, t)
        for c in range(2):
            cols = slice(c * dh, (c + 1) * dh)
            s_scr[c] = lax.dot_general(q_ref[:, cols], k_ref[pl.ds(start, t), cols], nt, preferred_element_type=F32)
        for c in range(2):
            for r0 in range(0, t, rs):
                rows = slice(r0, r0 + rs)
                s = s_scr[c, rows, :]
                if masked:
                    row = lax.broadcasted_iota(I32, (rs, t), 0) + r0
                    col = lax.broadcasted_iota(I32, (rs, t), 1)
                    s = jnp.where(col <= row, s, -jnp.inf)
                m_old = m_scr[c, rows, :]
                m_new = jnp.maximum(m_old, jnp.max(s, axis=-1, keepdims=True))
                alpha = jnp.exp2(m_old - m_new)
                p = jnp.exp2(s - jnp.tile(m_new, (1, t // LANES)))
                l_scr[c, rows, :] = alpha * l_scr[c, rows, :] + jnp.sum(p, axis=-1, keepdims=True)
                m_scr[c, rows, :] = m_new
                acc_scr[c, rows, :] = acc_scr[c, rows, :] * jnp.tile(alpha, (1, 2 * dh // LANES))
                p_scr[c, rows, :] = p.astype(p_scr.dtype)
        for c in range(2):
            acc_scr[c] += jnp.dot(p_scr[c], v_ref[pl.ds(start, t), :], preferred_element_type=F32)

    def body(j, _):
        chunk(j, False)
        return 0

    lax.fori_loop(0, i, body, 0)
    chunk(i, True)

    lam = (jnp.exp(jnp.sum(lq1_ref[...] * lk1_ref[...], axis=-1, keepdims=True))
           - jnp.exp(jnp.sum(lq2_ref[...] * lk2_ref[...], axis=-1, keepdims=True)) + lam_init)
    rep = (1, 2 * dh // LANES)
    o = acc_scr[0] / jnp.tile(l_scr[0], rep) - lam * (acc_scr[1] / jnp.tile(l_scr[1], rep))
    ms = jnp.mean(o * o, axis=-1, keepdims=True)
    y = o * lax.rsqrt(ms + EPS) * sg_ref[...] * (1.0 - lam_init)
    o_ref[...] = y.astype(o_ref.dtype)


def _diff_attention(proj, lq1, lk1, lq2, lk2, subln_g, n_heads, lam_init, t=1024):
    s = proj.shape[0]
    dh = ATT_HEAD_DIM
    hw = 2 * dh
    t = _tile(s, t)
    assert t % ATT_STRIP == 0 and t % LANES == 0
    vec = lambda h, i: (0, 0)
    once = pl.Buffered(1)
    return pl.pallas_call(
        functools.partial(_attn_body, t=t, lam_init=lam_init),
        grid=(n_heads, s // t),
        in_specs=[pl.BlockSpec((1, dh), vec), pl.BlockSpec((1, dh), vec), pl.BlockSpec((1, dh), vec),
                  pl.BlockSpec((1, dh), vec), pl.BlockSpec((1, hw), vec),
                  pl.BlockSpec((t, hw), lambda h, i: (i, h)),
                  pl.BlockSpec((s, hw), lambda h, i: (0, n_heads + h), pipeline_mode=once),
                  pl.BlockSpec((s, hw), lambda h, i: (0, 2 * n_heads + h), pipeline_mode=once)],
        out_specs=pl.BlockSpec((t, hw), lambda h, i: (i, h)),
        out_shape=jax.ShapeDtypeStruct((s, n_heads * hw), BF16),
        scratch_shapes=[pltpu.VMEM((2, t, t), F32), pltpu.VMEM((2, t, t), BF16), pltpu.VMEM((2, t, LANES), F32),
                        pltpu.VMEM((2, t, LANES), F32), pltpu.VMEM((2, t, hw), F32)],
        compiler_params=_params("parallel", "arbitrary"),
        name="diff_attention",
    )(lq1.reshape(1, dh), lk1.reshape(1, dh), lq2.reshape(1, dh), lk2.reshape(1, dh), subln_g.reshape(1, hw),
      proj, proj, proj)


def _causal_dwconv(w_ref, src_ref, n_taps, ts, width, finish):
    for r0 in range(0, ts, CONV_ROWS):
        acc = jnp.zeros((CONV_ROWS, width), F32)
        for j in range(n_taps):
            start = r0 + CONV_HALO - (n_taps - 1) + j
            acc = acc + w_ref[j:j + 1, :] * src_ref[start:start + CONV_ROWS, :]
        finish(r0, acc)


def _conv_body(ga_ref, gb_ref, sb_ref, scc_ref, sh_ref, ga_h, gb_h, scc_h, sh_h,
               dw_ref, db_ref, lng_ref, lnb_ref, scw_ref, conf_ref, sc_ref, u_scr, p_scr, *, ts):
    first = pl.program_id(0) == 0
    width = u_scr.shape[1]
    f32 = lambda r: r[...].astype(F32)
    u_scr[0:CONV_HALO, :] = jnp.where(first, 0.0, f32(ga_h) * jax.nn.sigmoid(f32(gb_h)))
    u_scr[CONV_HALO:CONV_HALO + ts, :] = f32(ga_ref) * jax.nn.sigmoid(f32(gb_ref))
    p_scr[0:CONV_HALO, :] = jnp.where(first, 0.0, f32(scc_h) * f32(sh_h))
    p_scr[CONV_HALO:CONV_HALO + ts, :] = f32(scc_ref) * f32(sh_ref)

    def finish_conf(r0, acc):
        u = acc + db_ref[...]
        mu = jnp.mean(u, axis=-1, keepdims=True)
        var = jnp.mean(jnp.square(u - mu), axis=-1, keepdims=True)
        y = (u - mu) * lax.rsqrt(var + EPS) * lng_ref[...] + lnb_ref[...]
        conf_ref[r0:r0 + CONV_ROWS, :] = (y * jax.nn.sigmoid(y)).astype(conf_ref.dtype)

    def finish_sc(r0, acc):
        sc_ref[r0:r0 + CONV_ROWS, :] = (sb_ref[r0:r0 + CONV_ROWS, :].astype(F32) * acc).astype(sc_ref.dtype)

    _causal_dwconv(dw_ref, u_scr, CONF_KERNEL, ts, width, finish_conf)
    _causal_dwconv(scw_ref, p_scr, SC_KERNEL, ts, width, finish_sc)


def _conv_modules(proj, dw_w, dw_b, ln_g, ln_b, sc_w, width, col0, ts=256):
    s = proj.shape[0]
    ts = _tile(s, ts)
    assert ts % CONV_HALO == 0 and CONV_HALO >= CONF_KERNEL - 1
    hb = ts // CONV_HALO
    cur = lambda c: pl.BlockSpec((ts, width), lambda i: (i, col0 + c))
    halo = lambda c: pl.BlockSpec((CONV_HALO, width), lambda i: (jnp.maximum(i * hb - 1, 0), col0 + c))
    par = lambda n: pl.BlockSpec((n, width), lambda i: (0, 0))
    out = pl.BlockSpec((ts, width), lambda i: (i, 0))
    return pl.pallas_call(
        functools.partial(_conv_body, ts=ts),
        grid=(s // ts,),
        in_specs=[cur(0), cur(1), cur(2), cur(3), cur(4), halo(0), halo(1), halo(3), halo(4),
                  par(CONF_KERNEL), par(1), par(1), par(1), par(SC_KERNEL)],
        out_specs=[out, out],
        out_shape=[jax.ShapeDtypeStruct((s, width), BF16), jax.ShapeDtypeStruct((s, width), BF16)],
        scratch_shapes=[pltpu.VMEM((CONV_HALO + ts, width), F32), pltpu.VMEM((CONV_HALO + ts, width), F32)],
        compiler_params=_params("parallel"),
        name="conv_modules",
    )(proj, proj, proj, proj, proj, proj, proj, proj, proj,
      dw_w, dw_b.reshape(1, width), ln_g.reshape(1, width), ln_b.reshape(1, width), sc_w)


def kernel(x, norm1_g, w_in, q_norm_g, k_norm_g, lam_q1, lam_k1, lam_q2, lam_k2, subln_g, conf_dw_w, conf_dw_b, conf_ln_g, conf_ln_b, sc_w, w_out, norm2_g, ffn_wg, ffn_wu, ffn_wd, router_w, moe_wg, moe_wu, moe_wd):
    b, s, d = x.shape
    depth = w_in.shape[0]
    dh = ATT_HEAD_DIM
    att_width = d // 2
    n_heads = att_width // (2 * dh)
    width = d // 4
    in_cols = w_in.shape[2]
    n_experts = router_w.shape[2]
    moe_tile = min(MOE_TILE, s)

    bf = lambda w: w.astype(BF16)
    w_in_b, w_out_b = bf(w_in), bf(w_out)
    ffn_wg_b, ffn_wu_b, ffn_wd_b = bf(ffn_wg), bf(ffn_wu), bf(ffn_wd)
    moe_wg_b, moe_wu_b, moe_wd_b = bf(moe_wg), bf(moe_wu), bf(moe_wd)

    ones_rest = jnp.ones((in_cols - 2 * att_width,), F32)
    qk_scale = jnp.concatenate([jnp.full((att_width,), dh ** -0.5 * LOG2E, F32), jnp.ones((in_cols - att_width,), F32)])

    outs = []
    for bi in range(b):
        xs = x[bi]
        h = None
        for l in range(depth):
            lam_init = 0.8 - 0.6 * math.exp(-0.3 * l)
            if h is None:
                h = _rmsnorm(xs, norm1_g[l])
            gain = jnp.concatenate([jnp.tile(q_norm_g[l], att_width // dh), jnp.tile(k_norm_g[l], att_width // dh), ones_rest])
            proj = _mm_in(h, w_in_b[l], gain.reshape(1, in_cols), qk_scale.reshape(1, in_cols), 2 * att_width)
            att = _diff_attention(proj, lam_q1[l], lam_k1[l], lam_q2[l], lam_k2[l], subln_g[l], n_heads, lam_init)
            conf, sc = _conv_modules(proj, conf_dw_w[l], conf_dw_b[l], conf_ln_g[l], conf_ln_b[l], sc_w[l],
                                     width, 3 * att_width // width)
            mix = jnp.concatenate([att, conf, sc], axis=-1)
            xs = _mm_resid(mix, w_out_b[l], xs)
            i = l // 2
            if l % 2 == 0:
                h = _rmsnorm(xs, norm2_g[l])
                hid = _mm_gated(h, ffn_wg_b[i], ffn_wu_b[i])
                xs = _mm_resid(hid, ffn_wd_b[i], xs)
                h = None
            else:
                h3, r_tab = _rmsnorm_router(xs, norm2_g[l], router_w[i])
                src, slot, tile_expert, n_live = _route(r_tab, n_experts, moe_tile)
                hid = _moe_up(h3, src, tile_expert, n_live, moe_wg_b[i], moe_wu_b[i], moe_tile)
                y3 = _moe_down(hid, tile_expert, n_live, moe_wd_b[i], moe_tile)
                xs, h = _moe_combine(xs, r_tab, slot, y3, norm1_g[l + 1] if l + 1 < depth else None)
        outs.append(xs)
    return jnp.stack(outs)
```

```python
import functools
import math

import jax
import jax.numpy as jnp
from jax import lax
from jax.experimental import pallas as pl
from jax.experimental.pallas import tpu as pltpu

EPS = 1e-6
BF16 = jnp.bfloat16
F32 = jnp.float32
I32 = jnp.int32

LANES = 128
SUBLANES = 8
VMEM_LIMIT = 56 * 1024 * 1024
ATT_HEAD_DIM = 128
CONF_KERNEL = 31
SC_KERNEL = 3
CONV_HALO = 32
CONV_ROWS = 16
ATT_STRIP = 32
ATT_UNITS = 2
MOE_TILE = 512
LOG2E = math.log2(math.e)


def _params(*sem):
    return pltpu.CompilerParams(dimension_semantics=sem, vmem_limit_bytes=VMEM_LIMIT)


def _tile(n, pref):
    t = min(n, pref)
    while n % t:
        t -= 8
    return t


def _rmsnorm_body(x_ref, g_ref, h_ref):
    x = x_ref[...]
    ms = jnp.mean(x * x, axis=-1, keepdims=True)
    h_ref[...] = (x * lax.rsqrt(ms + EPS) * g_ref[...]).astype(h_ref.dtype)


def _rmsnorm(x, g, tr=256):
    s, d = x.shape
    tr = _tile(s, tr)
    return pl.pallas_call(
        _rmsnorm_body,
        grid=(s // tr,),
        in_specs=[pl.BlockSpec((tr, d), lambda i: (i, 0)), pl.BlockSpec((1, d), lambda i: (0, 0))],
        out_specs=pl.BlockSpec((tr, d), lambda i: (i, 0)),
        out_shape=jax.ShapeDtypeStruct((s, d), BF16),
        compiler_params=_params("parallel"),
        name="rmsnorm",
    )(x, g.reshape(1, d))


def _rmsnorm_router_body(x_ref, g_ref, rw_ref, h_ref, r_ref, *, n_experts):
    x = x_ref[...]
    ms = jnp.mean(x * x, axis=-1, keepdims=True)
    y = x * lax.rsqrt(ms + EPS) * g_ref[...]
    h_ref[...] = y.astype(h_ref.dtype).reshape(h_ref.shape)
    logits = jnp.dot(y, rw_ref[...], preferred_element_type=F32, precision=lax.Precision.HIGHEST)
    lane = lax.broadcasted_iota(I32, logits.shape, 1).astype(F32)
    logits = jnp.where(lane < n_experts, logits, -jnp.inf)
    t1 = jnp.max(logits, axis=-1, keepdims=True)
    i1 = jnp.min(jnp.where(logits == t1, lane, float(LANES)), axis=-1, keepdims=True)
    rest = jnp.where(lane == i1, -jnp.inf, logits)
    t2 = jnp.max(rest, axis=-1, keepdims=True)
    i2 = jnp.min(jnp.where(rest == t2, lane, float(LANES)), axis=-1, keepdims=True)
    e2 = jnp.exp(t2 - t1)
    den = 1.0 + e2
    r_ref[...] = (jnp.where(lane == 0.0, i1, 0.0) + jnp.where(lane == 1.0, i2, 0.0)
                  + jnp.where(lane == 2.0, 1.0 / den, 0.0) + jnp.where(lane == 3.0, e2 / den, 0.0))


def _rmsnorm_router(x, g, router_w, tr=256):
    s, d = x.shape
    n_experts = router_w.shape[1]
    tr = _tile(s, tr)
    rw = jnp.pad(router_w, ((0, 0), (0, LANES - n_experts)))
    return pl.pallas_call(
        functools.partial(_rmsnorm_router_body, n_experts=n_experts),
        grid=(s // tr,),
        in_specs=[pl.BlockSpec((tr, d), lambda i: (i, 0)), pl.BlockSpec((1, d), lambda i: (0, 0)),
                  pl.BlockSpec((d, LANES), lambda i: (0, 0))],
        out_specs=[pl.BlockSpec((tr, d // LANES, LANES), lambda i: (i, 0, 0)), pl.BlockSpec((tr, LANES), lambda i: (i, 0))],
        out_shape=[jax.ShapeDtypeStruct((s, d // LANES, LANES), BF16), jax.ShapeDtypeStruct((s, LANES), F32)],
        compiler_params=_params("parallel"),
        name="rmsnorm_router",
    )(x, g.reshape(1, d), rw)


def _mm_in_body(a_ref, b_ref, gain_ref, scale_ref, o_ref, *, n_norm_tiles):
    acc = jnp.dot(a_ref[...], b_ref[...], preferred_element_type=F32)
    j = pl.program_id(1)

    @pl.when(j >= n_norm_tiles)
    def _():
        o_ref[...] = acc.astype(o_ref.dtype)

    @pl.when(j < n_norm_tiles)
    def _():
        for g in range(acc.shape[1] // ATT_HEAD_DIM):
            sl = slice(g * ATT_HEAD_DIM, (g + 1) * ATT_HEAD_DIM)
            blk = acc[:, sl]
            ms = jnp.mean(blk * blk, axis=-1, keepdims=True)
            y = blk * lax.rsqrt(ms + EPS) * gain_ref[:, sl] * scale_ref[:, sl]
            o_ref[:, sl] = y.astype(o_ref.dtype)


def _mm_in(h, w, layer, gain, scale, n_norm_cols, tm=1024):
    m, k = h.shape
    n = w.shape[2]
    tn = k // 4
    tm = _tile(m, tm)
    assert n % tn == 0 and n_norm_cols % tn == 0 and tn % ATT_HEAD_DIM == 0
    return pl.pallas_call(
        functools.partial(_mm_in_body, n_norm_tiles=n_norm_cols // tn),
        grid=(m // tm, n // tn),
        in_specs=[pl.BlockSpec((tm, k), lambda i, j: (i, 0)), pl.BlockSpec((None, k, tn), lambda i, j: (layer, 0, j)),
                  pl.BlockSpec((1, tn), lambda i, j: (0, j)), pl.BlockSpec((1, tn), lambda i, j: (0, j))],
        out_specs=pl.BlockSpec((tm, tn), lambda i, j: (i, j)),
        out_shape=jax.ShapeDtypeStruct((m, n), BF16),
        compiler_params=_params("parallel", "arbitrary"),
        name="in_proj",
    )(h, w, gain, scale)


def _mm_resid_body(a_ref, b_ref, r_ref, o_ref):
    d = jnp.dot(a_ref[...], b_ref[...], preferred_element_type=F32)
    kk = pl.program_id(2)

    @pl.when(kk == 0)
    def _():
        o_ref[...] = r_ref[...] + d

    @pl.when(kk > 0)
    def _():
        o_ref[...] += d


def _mm_resid(a, w, layer, r, tm=1024, tn=512, tk=4096):
    m, k = a.shape
    n = w.shape[2]
    tm, tn, tk = _tile(m, tm), _tile(n, tn), _tile(k, tk)
    return pl.pallas_call(
        _mm_resid_body,
        grid=(m // tm, n // tn, k // tk),
        in_specs=[pl.BlockSpec((tm, tk), lambda i, j, kk: (i, kk)), pl.BlockSpec((None, tk, tn), lambda i, j, kk: (layer, kk, j)),
                  pl.BlockSpec((tm, tn), lambda i, j, kk: (i, j))],
        out_specs=pl.BlockSpec((tm, tn), lambda i, j, kk: (i, j)),
        out_shape=jax.ShapeDtypeStruct((m, n), F32),
        compiler_params=_params("parallel", "parallel", "arbitrary"),
        name="mm_resid",
    )(a, w, r)


def _mm_gated_body(a_ref, wg_ref, wu_ref, o_ref):
    a = a_ref[...]
    g = jnp.dot(a, wg_ref[...], preferred_element_type=F32)
    u = jnp.dot(a, wu_ref[...], preferred_element_type=F32)
    o_ref[...] = (g * jax.nn.sigmoid(g) * u).astype(o_ref.dtype)


def _mm_gated(h, wg, wu, layer, tm=1024, tn=512):
    m, k = h.shape
    n = wg.shape[2]
    tm, tn = _tile(m, tm), _tile(n, tn)
    return pl.pallas_call(
        _mm_gated_body,
        grid=(m // tm, n // tn),
        in_specs=[pl.BlockSpec((tm, k), lambda i, j: (i, 0)), pl.BlockSpec((None, k, tn), lambda i, j: (layer, 0, j)),
                  pl.BlockSpec((None, k, tn), lambda i, j: (layer, 0, j))],
        out_specs=pl.BlockSpec((tm, tn), lambda i, j: (i, j)),
        out_shape=jax.ShapeDtypeStruct((m, n), BF16),
        compiler_params=_params("parallel", "arbitrary"),
        name="mm_gated",
    )(h, wg, wu)


def _route(r_tab, n_experts, tm):
    s = r_tab.shape[0]
    e_flat = jnp.concatenate([r_tab[:, 0], r_tab[:, 1]]).astype(I32)
    onehot = (e_flat[:, None] == jnp.arange(n_experts, dtype=I32)[None, :]).astype(I32)
    csum = jnp.cumsum(onehot, axis=0)
    rank = jnp.sum((csum - onehot) * onehot, axis=1)
    counts = csum[-1]
    padded = ((counts + tm - 1) // tm) * tm
    ends = jnp.cumsum(padded)
    dest = (ends - padded)[e_flat] + rank
    p_rows = 2 * s + n_experts * tm
    src = jnp.zeros((p_rows,), I32).at[dest].set(jnp.arange(2 * s, dtype=I32) % s)
    tile_ends = ends // tm
    tiles = jnp.arange(p_rows // tm, dtype=I32)
    tile_expert = jnp.minimum(jnp.sum((tiles[:, None] >= tile_ends[None, :]).astype(I32), axis=1), n_experts - 1)
    return src, dest.reshape(2, s), tile_expert, tile_ends[-1:].astype(I32)


def _row_copy(idx_ref, base, src_hbm, dst_ref, sem, r):
    return pltpu.make_async_copy(src_hbm.at[idx_ref[base + r]], dst_ref.at[r], sem)


def _gather_start(idx_ref, base, n, src_hbm, dst_ref, sem):
    def start(r, _):
        _row_copy(idx_ref, base, src_hbm, dst_ref, sem, r).start()
        return 0
    lax.fori_loop(0, n, start, 0)


def _gather_wait(idx_ref, base, n, src_hbm, dst_ref, sem):
    def wait(r, _):
        _row_copy(idx_ref, base, src_hbm, dst_ref, sem, r).wait()
        return 0
    lax.fori_loop(0, n, wait, 0)


def _moe_up_body(src_ref, te_ref, nv_ref, h3_hbm, wg_ref, wu_ref, o_ref, a3_scr, a_scr, sem, *, tm):
    i = pl.program_id(0)
    j = pl.program_id(1)
    n_live = nv_ref[0]
    live = i < n_live
    slot = i % 2

    @pl.when(jnp.logical_and(live, j == 0))
    def _():
        @pl.when(i == 0)
        def _():
            _gather_start(src_ref, 0, tm, h3_hbm, a3_scr.at[0], sem.at[0])

        @pl.when(i + 1 < n_live)
        def _():
            _gather_start(src_ref, (i + 1) * tm, tm, h3_hbm, a3_scr.at[1 - slot], sem.at[1 - slot])

        _gather_wait(src_ref, i * tm, tm, h3_hbm, a3_scr.at[slot], sem.at[slot])
        a_scr[...] = a3_scr[slot].reshape(a_scr.shape)

    @pl.when(live)
    def _():
        a = a_scr[...]
        g = jnp.dot(a, wg_ref[...], preferred_element_type=F32)
        u = jnp.dot(a, wu_ref[...], preferred_element_type=F32)
        o_ref[...] = (g * jax.nn.sigmoid(g) * u).astype(o_ref.dtype)

    @pl.when(jnp.logical_not(live))
    def _():
        o_ref[...] = jnp.zeros(o_ref.shape, o_ref.dtype)


def _moe_up(h3, src, tile_expert, n_live, wg, wu, layer, tm, tn=512):
    s, ns, nl = h3.shape
    d = ns * nl
    f = wg.shape[3]
    p_rows = src.shape[0]
    tn = _tile(f, tn)
    w_spec = pl.BlockSpec((None, None, d, tn), lambda i, j, src, te, nv: (layer, te[i], 0, j))
    return pl.pallas_call(
        functools.partial(_moe_up_body, tm=tm),
        grid_spec=pltpu.PrefetchScalarGridSpec(
            num_scalar_prefetch=3,
            grid=(p_rows // tm, f // tn),
            in_specs=[pl.BlockSpec(memory_space=pl.ANY), w_spec, w_spec],
            out_specs=pl.BlockSpec((tm, tn), lambda i, j, src, te, nv: (i, j)),
            scratch_shapes=[pltpu.VMEM((2, tm, ns, nl), BF16), pltpu.VMEM((tm, d), BF16), pltpu.SemaphoreType.DMA((2,))]),
        out_shape=jax.ShapeDtypeStruct((p_rows, f), BF16),
        compiler_params=_params("arbitrary", "arbitrary"),
        name="moe_up",
    )(src, tile_expert, n_live, h3, wg, wu)


def _moe_down_body(te_ref, nv_ref, a_ref, w_ref, o_ref):
    live = pl.program_id(0) < nv_ref[0]

    @pl.when(live)
    def _():
        o_ref[...] = jnp.dot(a_ref[...], w_ref[...], preferred_element_type=F32).reshape(o_ref.shape)

    @pl.when(jnp.logical_not(live))
    def _():
        o_ref[...] = jnp.zeros(o_ref.shape, o_ref.dtype)


def _moe_down(hid, tile_expert, n_live, wd, layer, tm, tn=1024):
    p_rows, f = hid.shape
    d = wd.shape[3]
    tn = _tile(d, tn)
    assert tn % (8 * LANES) == 0
    return pl.pallas_call(
        _moe_down_body,
        grid_spec=pltpu.PrefetchScalarGridSpec(
            num_scalar_prefetch=2,
            grid=(p_rows // tm, d // tn),
            in_specs=[pl.BlockSpec((tm, f), lambda i, j, te, nv: (i, 0)),
                      pl.BlockSpec((None, None, f, tn), lambda i, j, te, nv: (layer, te[i], 0, j))],
            out_specs=pl.BlockSpec((tm, tn // LANES, LANES), lambda i, j, te, nv: (i, j, 0))),
        out_shape=jax.ShapeDtypeStruct((p_rows, d // LANES, LANES), F32),
        compiler_params=_params("arbitrary", "arbitrary"),
        name="moe_down",
    )(tile_expert, n_live, hid, wd)


def _moe_combine_body(slot_ref, x_ref, r_ref, g_ref, y3_hbm, xo_ref, *rest, tc, s, norm):
    if norm:
        h_ref, b1, b2, sem = rest
    else:
        b1, b2, sem = rest
    i = pl.program_id(0)
    cur = i % 2

    def start(step, buf):
        _gather_start(slot_ref, step * tc, tc, y3_hbm, b1.at[buf], sem.at[0, buf])
        _gather_start(slot_ref, s + step * tc, tc, y3_hbm, b2.at[buf], sem.at[1, buf])

    @pl.when(i == 0)
    def _():
        start(0, 0)

    @pl.when(i + 1 < pl.num_programs(0))
    def _():
        start(i + 1, 1 - cur)

    _gather_wait(slot_ref, i * tc, tc, y3_hbm, b1.at[cur], sem.at[0, cur])
    _gather_wait(slot_ref, s + i * tc, tc, y3_hbm, b2.at[cur], sem.at[1, cur])
    r = r_ref[...]
    xn = x_ref[...] + r[:, 2:3] * b1[cur].reshape(x_ref.shape) + r[:, 3:4] * b2[cur].reshape(x_ref.shape)
    xo_ref[...] = xn
    if norm:
        ms = jnp.mean(xn * xn, axis=-1, keepdims=True)
        h_ref[...] = (xn * lax.rsqrt(ms + EPS) * g_ref[...]).astype(h_ref.dtype)


def _moe_combine(x, r_tab, slot, y3, next_g, tc=256):
    s, d = x.shape
    tc = _tile(s, tc)
    ns, nl = y3.shape[1:]
    norm = next_g is not None
    g = (next_g if norm else jnp.ones((d,), F32)).reshape(1, d)
    row = pl.BlockSpec((tc, d), lambda i, slot: (i, 0))
    out_specs = [row, row] if norm else [row]
    out_shape = [jax.ShapeDtypeStruct((s, d), F32)] + ([jax.ShapeDtypeStruct((s, d), BF16)] if norm else [])
    res = pl.pallas_call(
        functools.partial(_moe_combine_body, tc=tc, s=s, norm=norm),
        grid_spec=pltpu.PrefetchScalarGridSpec(
            num_scalar_prefetch=1,
            grid=(s // tc,),
            in_specs=[row, pl.BlockSpec((tc, LANES), lambda i, slot: (i, 0)), pl.BlockSpec((1, d), lambda i, slot: (0, 0)),
                      pl.BlockSpec(memory_space=pl.ANY)],
            out_specs=out_specs,
            scratch_shapes=[pltpu.VMEM((2, tc, ns, nl), F32), pltpu.VMEM((2, tc, ns, nl), F32), pltpu.SemaphoreType.DMA((2, 2))]),
        out_shape=out_shape,
        compiler_params=_params("arbitrary"),
        name="moe_combine",
    )(slot.reshape(2 * s), x, r_tab, g, y3)
    return (res[0], res[1]) if norm else (res[0], None)


def _attn_body(lq1_ref, lk1_ref, lq2_ref, lk2_ref, sg_ref, q_ref, k_ref, v_ref, o_ref,
               s_scr, p_scr, m_scr, l_scr, acc_scr, *, t, lam_init):
    dh = ATT_HEAD_DIM
    rs = ATT_STRIP
    i = pl.program_id(1)
    nt = (((1,), (1,)), ((), ()))

    m_scr[...] = jnp.full(m_scr.shape, -jnp.inf, F32)
    l_scr[...] = jnp.zeros(l_scr.shape, F32)
    acc_scr[...] = jnp.zeros(acc_scr.shape, F32)

    def chunk(j, masked):
        start = pl.multiple_of(j * t, t)
        tu = t // ATT_UNITS
        units = [(c, u) for u in range(ATT_UNITS) for c in range(2)]
        n_keys = lambda u: (u + 1) * tu if masked else t
        for c, u in units:
            cols = slice(c * dh, (c + 1) * dh)
            rows_u = slice(u * tu, (u + 1) * tu)
            s_scr[c, rows_u, 0:n_keys(u)] = lax.dot_general(q_ref[rows_u, cols], k_ref[pl.ds(start, n_keys(u)), cols], nt,
                                                           preferred_element_type=F32)
        for c, u in units:
            nk = n_keys(u)
            for r0 in range(u * tu, (u + 1) * tu, rs):
                rows = slice(r0, r0 + rs)
                s = s_scr[c, rows, 0:nk]
                if masked:
                    row = lax.broadcasted_iota(I32, (rs, nk), 0) + r0
                    col = lax.broadcasted_iota(I32, (rs, nk), 1)
                    s = jnp.where(col <= row, s, -jnp.inf)
                m_old = m_scr[c, rows, :]
                m_new = jnp.maximum(m_old, jnp.max(s, axis=-1, keepdims=True))
                alpha = jnp.exp2(m_old - m_new)
                p = jnp.exp2(s - jnp.tile(m_new, (1, nk // LANES)))
                l_scr[c, rows, :] = alpha * l_scr[c, rows, :] + jnp.sum(p, axis=-1, keepdims=True)
                m_scr[c, rows, :] = m_new
                acc_scr[c, rows, :] = acc_scr[c, rows, :] * jnp.tile(alpha, (1, 2 * dh // LANES))
                p_scr[c, rows, 0:nk] = p.astype(p_scr.dtype)
            rows_u = slice(u * tu, (u + 1) * tu)
            acc_scr[c, rows_u, :] += jnp.dot(p_scr[c, rows_u, 0:nk], v_ref[pl.ds(start, nk), :], preferred_element_type=F32)

    def body(j, _):
        chunk(j, False)
        return 0

    lax.fori_loop(0, i, body, 0)
    chunk(i, True)

    lam = (jnp.exp(jnp.sum(lq1_ref[...] * lk1_ref[...], axis=-1, keepdims=True))
           - jnp.exp(jnp.sum(lq2_ref[...] * lk2_ref[...], axis=-1, keepdims=True)) + lam_init)
    rep = (1, 2 * dh // LANES)
    o = acc_scr[0] / jnp.tile(l_scr[0], rep) - lam * (acc_scr[1] / jnp.tile(l_scr[1], rep))
    ms = jnp.mean(o * o, axis=-1, keepdims=True)
    y = o * lax.rsqrt(ms + EPS) * sg_ref[...] * (1.0 - lam_init)
    o_ref[...] = y.astype(o_ref.dtype)


def _diff_attention(proj, lq1, lk1, lq2, lk2, subln_g, n_heads, lam_init, t=1024):
    s = proj.shape[0]
    dh = ATT_HEAD_DIM
    hw = 2 * dh
    t = _tile(s, t)
    assert t % (ATT_UNITS * ATT_STRIP) == 0 and t % (ATT_UNITS * LANES) == 0
    vec = lambda h, i: (0, 0)
    once = pl.Buffered(1)
    return pl.pallas_call(
        functools.partial(_attn_body, t=t, lam_init=lam_init),
        grid=(n_heads, s // t),
        in_specs=[pl.BlockSpec((1, dh), vec), pl.BlockSpec((1, dh), vec), pl.BlockSpec((1, dh), vec),
                  pl.BlockSpec((1, dh), vec), pl.BlockSpec((1, hw), vec),
                  pl.BlockSpec((t, hw), lambda h, i: (i, h)),
                  pl.BlockSpec((s, hw), lambda h, i: (0, n_heads + h), pipeline_mode=once),
                  pl.BlockSpec((s, hw), lambda h, i: (0, 2 * n_heads + h), pipeline_mode=once)],
        out_specs=pl.BlockSpec((t, hw), lambda h, i: (i, h)),
        out_shape=jax.ShapeDtypeStruct((s, n_heads * hw), BF16),
        scratch_shapes=[pltpu.VMEM((2, t, t), F32), pltpu.VMEM((2, t, t), BF16), pltpu.VMEM((2, t, LANES), F32),
                        pltpu.VMEM((2, t, LANES), F32), pltpu.VMEM((2, t, hw), F32)],
        compiler_params=_params("parallel", "arbitrary"),
        name="diff_attention",
    )(lq1.reshape(1, dh), lk1.reshape(1, dh), lq2.reshape(1, dh), lk2.reshape(1, dh), subln_g.reshape(1, hw),
      proj, proj, proj)


def _tap_offsets(n_taps):
    return sorted({(CONV_HALO - (n_taps - 1) + j) % SUBLANES for j in range(n_taps)} - {0})


def _causal_dwconv(w_ref, src_ref, shift_ref, n_taps, ts, width, finish):
    offs = _tap_offsets(n_taps)
    span = CONV_HALO + ts - SUBLANES
    for k, off in enumerate(offs):
        shift_ref[k, 0:span, :] = src_ref[off:off + span, :]
    for r0 in range(0, ts, CONV_ROWS):
        acc = jnp.zeros((CONV_ROWS, width), F32)
        for j in range(n_taps):
            start = r0 + CONV_HALO - (n_taps - 1) + j
            off, base = start % SUBLANES, start - start % SUBLANES
            if off == 0:
                tap = src_ref[base:base + CONV_ROWS, :]
            else:
                tap = shift_ref[offs.index(off), base:base + CONV_ROWS, :]
            acc = acc + w_ref[j:j + 1, :] * tap
        finish(r0, acc)


def _conv_body(ga_ref, gb_ref, sb_ref, scc_ref, sh_ref, ga_h, gb_h, scc_h, sh_h,
               dw_ref, db_ref, lng_ref, lnb_ref, scw_ref, conf_ref, sc_ref, u_scr, p_scr, us_scr, ps_scr, *, ts):
    first = pl.program_id(0) == 0
    width = u_scr.shape[1]
    f32 = lambda r: r[...].astype(F32)
    u_scr[0:CONV_HALO, :] = jnp.where(first, 0.0, f32(ga_h) * jax.nn.sigmoid(f32(gb_h)))
    u_scr[CONV_HALO:CONV_HALO + ts, :] = f32(ga_ref) * jax.nn.sigmoid(f32(gb_ref))
    p_scr[0:CONV_HALO, :] = jnp.where(first, 0.0, f32(scc_h) * f32(sh_h))
    p_scr[CONV_HALO:CONV_HALO + ts, :] = f32(scc_ref) * f32(sh_ref)

    def finish_conf(r0, acc):
        u = acc + db_ref[...]
        mu = jnp.mean(u, axis=-1, keepdims=True)
        var = jnp.mean(jnp.square(u - mu), axis=-1, keepdims=True)
        y = (u - mu) * lax.rsqrt(var + EPS) * lng_ref[...] + lnb_ref[...]
        conf_ref[r0:r0 + CONV_ROWS, :] = (y * jax.nn.sigmoid(y)).astype(conf_ref.dtype)

    def finish_sc(r0, acc):
        sc_ref[r0:r0 + CONV_ROWS, :] = (sb_ref[r0:r0 + CONV_ROWS, :].astype(F32) * acc).astype(sc_ref.dtype)

    _causal_dwconv(dw_ref, u_scr, us_scr, CONF_KERNEL, ts, width, finish_conf)
    _causal_dwconv(scw_ref, p_scr, ps_scr, SC_KERNEL, ts, width, finish_sc)


def _conv_modules(proj, dw_w, dw_b, ln_g, ln_b, sc_w, width, col0, ts=256):
    s = proj.shape[0]
    ts = _tile(s, ts)
    assert ts % CONV_HALO == 0 and CONV_HALO >= CONF_KERNEL - 1
    hb = ts // CONV_HALO
    cur = lambda c: pl.BlockSpec((ts, width), lambda i: (i, col0 + c))
    halo = lambda c: pl.BlockSpec((CONV_HALO, width), lambda i: (jnp.maximum(i * hb - 1, 0), col0 + c))
    par = lambda n: pl.BlockSpec((n, width), lambda i: (0, 0))
    out = pl.BlockSpec((ts, width), lambda i: (i, 0))
    return pl.pallas_call(
        functools.partial(_conv_body, ts=ts),
        grid=(s // ts,),
        in_specs=[cur(0), cur(1), cur(2), cur(3), cur(4), halo(0), halo(1), halo(3), halo(4),
                  par(CONF_KERNEL), par(1), par(1), par(1), par(SC_KERNEL)],
        out_specs=[out, out],
        out_shape=[jax.ShapeDtypeStruct((s, width), BF16), jax.ShapeDtypeStruct((s, width), BF16)],
        scratch_shapes=[pltpu.VMEM((CONV_HALO + ts, width), F32), pltpu.VMEM((CONV_HALO + ts, width), F32),
                        pltpu.VMEM((len(_tap_offsets(CONF_KERNEL)), CONV_HALO + ts, width), F32),
                        pltpu.VMEM((len(_tap_offsets(SC_KERNEL)), CONV_HALO + ts, width), F32)],
        compiler_params=_params("parallel"),
        name="conv_modules",
    )(proj, proj, proj, proj, proj, proj, proj, proj, proj,
      dw_w, dw_b.reshape(1, width), ln_g.reshape(1, width), ln_b.reshape(1, width), sc_w)


def kernel(x, norm1_g, w_in, q_norm_g, k_norm_g, lam_q1, lam_k1, lam_q2, lam_k2, subln_g, conf_dw_w, conf_dw_b, conf_ln_g, conf_ln_b, sc_w, w_out, norm2_g, ffn_wg, ffn_wu, ffn_wd, router_w, moe_wg, moe_wu, moe_wd):
    b, s, d = x.shape
    depth = w_in.shape[0]
    dh = ATT_HEAD_DIM
    att_width = d // 2
    n_heads = att_width // (2 * dh)
    width = d // 4
    in_cols = w_in.shape[2]
    n_experts = router_w.shape[2]
    moe_tile = min(MOE_TILE, s)

    bf = lambda w: w.astype(BF16)
    w_in_b, w_out_b = bf(w_in), bf(w_out)
    ffn_wg_b, ffn_wu_b, ffn_wd_b = bf(ffn_wg), bf(ffn_wu), bf(ffn_wd)
    moe_wg_b, moe_wu_b, moe_wd_b = bf(moe_wg), bf(moe_wu), bf(moe_wd)

    ones_rest = jnp.ones((in_cols - 2 * att_width,), F32)
    qk_scale = jnp.concatenate([jnp.full((att_width,), dh ** -0.5 * LOG2E, F32), jnp.ones((in_cols - att_width,), F32)])

    outs = []
    for bi in range(b):
        xs = x[bi]
        h = None
        for l in range(depth):
            lam_init = 0.8 - 0.6 * math.exp(-0.3 * l)
            if h is None:
                h = _rmsnorm(xs, norm1_g[l])
            gain = jnp.concatenate([jnp.tile(q_norm_g[l], att_width // dh), jnp.tile(k_norm_g[l], att_width // dh), ones_rest])
            proj = _mm_in(h, w_in_b, l, gain.reshape(1, in_cols), qk_scale.reshape(1, in_cols), 2 * att_width)
            att = _diff_attention(proj, lam_q1[l], lam_k1[l], lam_q2[l], lam_k2[l], subln_g[l], n_heads, lam_init)
            conf, sc = _conv_modules(proj, conf_dw_w[l], conf_dw_b[l], conf_ln_g[l], conf_ln_b[l], sc_w[l],
                                     width, 3 * att_width // width)
            mix = jnp.concatenate([att, conf, sc], axis=-1)
            xs = _mm_resid(mix, w_out_b, l, xs)
            i = l // 2
            if l % 2 == 0:
                h = _rmsnorm(xs, norm2_g[l])
                hid = _mm_gated(h, ffn_wg_b, ffn_wu_b, i)
                xs = _mm_resid(hid, ffn_wd_b, i, xs)
                h = None
            else:
                h3, r_tab = _rmsnorm_router(xs, norm2_g[l], router_w[i])
                src, slot, tile_expert, n_live = _route(r_tab, n_experts, moe_tile)
                hid = _moe_up(h3, src, tile_expert, n_live, moe_wg_b, moe_wu_b, i, moe_tile)
                y3 = _moe_down(hid, tile_expert, n_live, moe_wd_b, i, moe_tile)
                xs, h = _moe_combine(xs, r_tab, slot, y3, norm1_g[l + 1] if l + 1 < depth else None)
        outs.append(xs)
    return jnp.stack(outs)
```

```python
import functools
import math

import jax
import jax.numpy as jnp
from jax import lax
from jax.experimental import pallas as pl
from jax.experimental.pallas import tpu as pltpu

EPS = 1e-6
BF16 = jnp.bfloat16
F32 = jnp.float32
I32 = jnp.int32

LANES = 128
SUBLANES = 8
VMEM_LIMIT = 56 * 1024 * 1024
ATT_HEAD_DIM = 128
CONF_KERNEL = 31
SC_KERNEL = 3
CONV_HALO = 32
CONV_ROWS = 16
ATT_STRIP = 32
ATT_UNITS = 2
MOE_TILE = 512
MOE_UP_COLS = 512
MOE_DOWN_COLS = 1024
LOG2E = math.log2(math.e)


def _params(*sem):
    return pltpu.CompilerParams(dimension_semantics=sem, vmem_limit_bytes=VMEM_LIMIT)


def _tile(n, pref):
    t = min(n, pref)
    while n % t:
        t -= 8
    return t


def _rmsnorm_body(x_ref, g_ref, h_ref):
    x = x_ref[...]
    ms = jnp.mean(x * x, axis=-1, keepdims=True)
    h_ref[...] = (x * lax.rsqrt(ms + EPS) * g_ref[...]).astype(h_ref.dtype)


def _rmsnorm(x, g, tr=256):
    s, d = x.shape
    tr = _tile(s, tr)
    return pl.pallas_call(
        _rmsnorm_body,
        grid=(s // tr,),
        in_specs=[pl.BlockSpec((tr, d), lambda i: (i, 0)), pl.BlockSpec((1, d), lambda i: (0, 0))],
        out_specs=pl.BlockSpec((tr, d), lambda i: (i, 0)),
        out_shape=jax.ShapeDtypeStruct((s, d), BF16),
        compiler_params=_params("parallel"),
        name="rmsnorm",
    )(x, g.reshape(1, d))


def _rmsnorm_router_body(x_ref, g_ref, rw_ref, h_ref, r_ref, *, n_experts):
    x = x_ref[...]
    ms = jnp.mean(x * x, axis=-1, keepdims=True)
    y = x * lax.rsqrt(ms + EPS) * g_ref[...]
    h_ref[...] = y.astype(h_ref.dtype).reshape(h_ref.shape)
    logits = jnp.dot(y, rw_ref[...], preferred_element_type=F32, precision=lax.Precision.HIGHEST)
    lane = lax.broadcasted_iota(I32, logits.shape, 1).astype(F32)
    logits = jnp.where(lane < n_experts, logits, -jnp.inf)
    t1 = jnp.max(logits, axis=-1, keepdims=True)
    i1 = jnp.min(jnp.where(logits == t1, lane, float(LANES)), axis=-1, keepdims=True)
    rest = jnp.where(lane == i1, -jnp.inf, logits)
    t2 = jnp.max(rest, axis=-1, keepdims=True)
    i2 = jnp.min(jnp.where(rest == t2, lane, float(LANES)), axis=-1, keepdims=True)
    e2 = jnp.exp(t2 - t1)
    den = 1.0 + e2
    r_ref[...] = (jnp.where(lane == 0.0, i1, 0.0) + jnp.where(lane == 1.0, i2, 0.0)
                  + jnp.where(lane == 2.0, 1.0 / den, 0.0) + jnp.where(lane == 3.0, e2 / den, 0.0))


def _rmsnorm_router(x, g, router_w, tr=256):
    s, d = x.shape
    n_experts = router_w.shape[1]
    tr = _tile(s, tr)
    rw = jnp.pad(router_w, ((0, 0), (0, LANES - n_experts)))
    return pl.pallas_call(
        functools.partial(_rmsnorm_router_body, n_experts=n_experts),
        grid=(s // tr,),
        in_specs=[pl.BlockSpec((tr, d), lambda i: (i, 0)), pl.BlockSpec((1, d), lambda i: (0, 0)),
                  pl.BlockSpec((d, LANES), lambda i: (0, 0))],
        out_specs=[pl.BlockSpec((tr, d // LANES, LANES), lambda i: (i, 0, 0)), pl.BlockSpec((tr, LANES), lambda i: (i, 0))],
        out_shape=[jax.ShapeDtypeStruct((s, d // LANES, LANES), BF16), jax.ShapeDtypeStruct((s, LANES), F32)],
        compiler_params=_params("parallel"),
        name="rmsnorm_router",
    )(x, g.reshape(1, d), rw)


def _mm_in_body(a_ref, b_ref, gain_ref, scale_ref, o_ref, *, n_norm_tiles):
    acc = jnp.dot(a_ref[...], b_ref[...], preferred_element_type=F32)
    j = pl.program_id(1)

    @pl.when(j >= n_norm_tiles)
    def _():
        o_ref[...] = acc.astype(o_ref.dtype)

    @pl.when(j < n_norm_tiles)
    def _():
        for g in range(acc.shape[1] // ATT_HEAD_DIM):
            sl = slice(g * ATT_HEAD_DIM, (g + 1) * ATT_HEAD_DIM)
            blk = acc[:, sl]
            ms = jnp.mean(blk * blk, axis=-1, keepdims=True)
            y = blk * lax.rsqrt(ms + EPS) * gain_ref[:, sl] * scale_ref[:, sl]
            o_ref[:, sl] = y.astype(o_ref.dtype)


def _mm_in(h, w, layer, gain, scale, n_norm_cols, tm=1024):
    m, k = h.shape
    n = w.shape[2]
    tn = k // 4
    tm = _tile(m, tm)
    assert n % tn == 0 and n_norm_cols % tn == 0 and tn % ATT_HEAD_DIM == 0
    return pl.pallas_call(
        functools.partial(_mm_in_body, n_norm_tiles=n_norm_cols // tn),
        grid=(m // tm, n // tn),
        in_specs=[pl.BlockSpec((tm, k), lambda i, j: (i, 0)), pl.BlockSpec((None, k, tn), lambda i, j: (layer, 0, j)),
                  pl.BlockSpec((1, tn), lambda i, j: (0, j)), pl.BlockSpec((1, tn), lambda i, j: (0, j))],
        out_specs=pl.BlockSpec((tm, tn), lambda i, j: (i, j)),
        out_shape=jax.ShapeDtypeStruct((m, n), BF16),
        compiler_params=_params("parallel", "arbitrary"),
        name="in_proj",
    )(h, w, gain, scale)


def _mm_resid_body(a_ref, b_ref, r_ref, o_ref):
    d = jnp.dot(a_ref[...], b_ref[...], preferred_element_type=F32)
    kk = pl.program_id(2)

    @pl.when(kk == 0)
    def _():
        o_ref[...] = r_ref[...] + d

    @pl.when(kk > 0)
    def _():
        o_ref[...] += d


def _mm_resid(a, w, layer, r, tm=1024, tn=512, tk=4096):
    m, k = a.shape
    n = w.shape[2]
    tm, tn, tk = _tile(m, tm), _tile(n, tn), _tile(k, tk)
    return pl.pallas_call(
        _mm_resid_body,
        grid=(m // tm, n // tn, k // tk),
        in_specs=[pl.BlockSpec((tm, tk), lambda i, j, kk: (i, kk)), pl.BlockSpec((None, tk, tn), lambda i, j, kk: (layer, kk, j)),
                  pl.BlockSpec((tm, tn), lambda i, j, kk: (i, j))],
        out_specs=pl.BlockSpec((tm, tn), lambda i, j, kk: (i, j)),
        out_shape=jax.ShapeDtypeStruct((m, n), F32),
        compiler_params=_params("parallel", "parallel", "arbitrary"),
        name="mm_resid",
    )(a, w, r)


def _mm_gated_body(a_ref, wg_ref, wu_ref, o_ref):
    a = a_ref[...]
    g = jnp.dot(a, wg_ref[...], preferred_element_type=F32)
    u = jnp.dot(a, wu_ref[...], preferred_element_type=F32)
    o_ref[...] = (g * jax.nn.sigmoid(g) * u).astype(o_ref.dtype)


def _mm_gated(h, wg, wu, layer, tm=1024, tn=512):
    m, k = h.shape
    n = wg.shape[2]
    tm, tn = _tile(m, tm), _tile(n, tn)
    return pl.pallas_call(
        _mm_gated_body,
        grid=(m // tm, n // tn),
        in_specs=[pl.BlockSpec((tm, k), lambda i, j: (i, 0)), pl.BlockSpec((None, k, tn), lambda i, j: (layer, 0, j)),
                  pl.BlockSpec((None, k, tn), lambda i, j: (layer, 0, j))],
        out_specs=pl.BlockSpec((tm, tn), lambda i, j: (i, j)),
        out_shape=jax.ShapeDtypeStruct((m, n), BF16),
        compiler_params=_params("parallel", "arbitrary"),
        name="mm_gated",
    )(h, wg, wu)


def _route(r_tab, n_experts, tm):
    s = r_tab.shape[0]
    e_flat = jnp.concatenate([r_tab[:, 0], r_tab[:, 1]]).astype(I32)
    onehot = (e_flat[:, None] == jnp.arange(n_experts, dtype=I32)[None, :]).astype(I32)
    csum = jnp.cumsum(onehot, axis=0)
    rank = jnp.sum((csum - onehot) * onehot, axis=1)
    counts = csum[-1]
    padded = ((counts + tm - 1) // tm) * tm
    ends = jnp.cumsum(padded)
    dest = (ends - padded)[e_flat] + rank
    p_rows = 2 * s + n_experts * tm
    src = jnp.zeros((p_rows,), I32).at[dest].set(jnp.arange(2 * s, dtype=I32) % s)
    tile_ends = ends // tm
    tiles = jnp.arange(p_rows // tm, dtype=I32)
    tile_expert = jnp.minimum(jnp.sum((tiles[:, None] >= tile_ends[None, :]).astype(I32), axis=1), n_experts - 1)
    return src, dest.reshape(2, s), tile_expert, tile_ends[-1:].astype(I32)


def _row_copy(idx_ref, base, src_hbm, dst_ref, sem, r):
    return pltpu.make_async_copy(src_hbm.at[idx_ref[base + r]], dst_ref.at[r], sem)


def _gather_start(idx_ref, base, n, src_hbm, dst_ref, sem):
    def start(r, _):
        _row_copy(idx_ref, base, src_hbm, dst_ref, sem, r).start()
        return 0
    lax.fori_loop(0, n, start, 0)


def _gather_wait(idx_ref, base, n, src_hbm, dst_ref, sem):
    def wait(r, _):
        _row_copy(idx_ref, base, src_hbm, dst_ref, sem, r).wait()
        return 0
    lax.fori_loop(0, n, wait, 0)


def _moe_up_body(src_ref, te_ref, nv_ref, h3_hbm, wg_ref, wu_ref, o_ref, a3_scr, a_scr, sem, *, tm):
    i = pl.program_id(0)
    j = pl.program_id(1)
    n_live = nv_ref[0]
    live = i < n_live
    slot = i % 2

    @pl.when(jnp.logical_and(live, j == 0))
    def _():
        @pl.when(i == 0)
        def _():
            _gather_start(src_ref, 0, tm, h3_hbm, a3_scr.at[0], sem.at[0])

        @pl.when(i + 1 < n_live)
        def _():
            _gather_start(src_ref, (i + 1) * tm, tm, h3_hbm, a3_scr.at[1 - slot], sem.at[1 - slot])

        _gather_wait(src_ref, i * tm, tm, h3_hbm, a3_scr.at[slot], sem.at[slot])
        a_scr[...] = a3_scr[slot].reshape(a_scr.shape)

    @pl.when(live)
    def _():
        a = a_scr[...]
        g = jnp.dot(a, wg_ref[...], preferred_element_type=F32)
        u = jnp.dot(a, wu_ref[...], preferred_element_type=F32)
        o_ref[...] = (g * jax.nn.sigmoid(g) * u).astype(o_ref.dtype)

    @pl.when(jnp.logical_not(live))
    def _():
        o_ref[...] = jnp.zeros(o_ref.shape, o_ref.dtype)


def _col_blocked(w, tn):
    *lead, k, n = w.shape
    nl = len(lead)
    return w.reshape(*lead, k, n // tn, tn).transpose(*range(nl), nl + 1, nl, nl + 2)


def _moe_up(h3, src, tile_expert, n_live, wg, wu, layer, tm):
    s, ns, nl = h3.shape
    d = ns * nl
    tn = wg.shape[4]
    f = wg.shape[2] * tn
    p_rows = src.shape[0]
    w_spec = pl.BlockSpec((None, None, None, d, tn), lambda i, j, src, te, nv: (layer, te[i], j, 0, 0))
    return pl.pallas_call(
        functools.partial(_moe_up_body, tm=tm),
        grid_spec=pltpu.PrefetchScalarGridSpec(
            num_scalar_prefetch=3,
            grid=(p_rows // tm, f // tn),
            in_specs=[pl.BlockSpec(memory_space=pl.ANY), w_spec, w_spec],
            out_specs=pl.BlockSpec((tm, tn), lambda i, j, src, te, nv: (i, j)),
            scratch_shapes=[pltpu.VMEM((2, tm, ns, nl), BF16), pltpu.VMEM((tm, d), BF16), pltpu.SemaphoreType.DMA((2,))]),
        out_shape=jax.ShapeDtypeStruct((p_rows, f), BF16),
        compiler_params=_params("arbitrary", "arbitrary"),
        name="moe_up",
    )(src, tile_expert, n_live, h3, wg, wu)


def _moe_down_body(te_ref, nv_ref, a_ref, w_ref, o_ref):
    live = pl.program_id(0) < nv_ref[0]

    @pl.when(live)
    def _():
        o_ref[...] = jnp.dot(a_ref[...], w_ref[...], preferred_element_type=F32).reshape(o_ref.shape)

    @pl.when(jnp.logical_not(live))
    def _():
        o_ref[...] = jnp.zeros(o_ref.shape, o_ref.dtype)


def _moe_down(hid, tile_expert, n_live, wd, layer, tm):
    p_rows, f = hid.shape
    tn = wd.shape[4]
    d = wd.shape[2] * tn
    assert tn % (8 * LANES) == 0
    return pl.pallas_call(
        _moe_down_body,
        grid_spec=pltpu.PrefetchScalarGridSpec(
            num_scalar_prefetch=2,
            grid=(p_rows // tm, d // tn),
            in_specs=[pl.BlockSpec((tm, f), lambda i, j, te, nv: (i, 0)),
                      pl.BlockSpec((None, None, None, f, tn), lambda i, j, te, nv: (layer, te[i], j, 0, 0))],
            out_specs=pl.BlockSpec((tm, tn // LANES, LANES), lambda i, j, te, nv: (i, j, 0))),
        out_shape=jax.ShapeDtypeStruct((p_rows, d // LANES, LANES), F32),
        compiler_params=_params("arbitrary", "arbitrary"),
        name="moe_down",
    )(tile_expert, n_live, hid, wd)


def _moe_combine_body(slot_ref, x_ref, r_ref, g_ref, y3_hbm, xo_ref, *rest, tc, s, norm):
    if norm:
        h_ref, b1, b2, sem = rest
    else:
        b1, b2, sem = rest
    i = pl.program_id(0)
    cur = i % 2

    def start(step, buf):
        _gather_start(slot_ref, step * tc, tc, y3_hbm, b1.at[buf], sem.at[0, buf])
        _gather_start(slot_ref, s + step * tc, tc, y3_hbm, b2.at[buf], sem.at[1, buf])

    @pl.when(i == 0)
    def _():
        start(0, 0)

    @pl.when(i + 1 < pl.num_programs(0))
    def _():
        start(i + 1, 1 - cur)

    _gather_wait(slot_ref, i * tc, tc, y3_hbm, b1.at[cur], sem.at[0, cur])
    _gather_wait(slot_ref, s + i * tc, tc, y3_hbm, b2.at[cur], sem.at[1, cur])
    r = r_ref[...]
    xn = x_ref[...] + r[:, 2:3] * b1[cur].reshape(x_ref.shape) + r[:, 3:4] * b2[cur].reshape(x_ref.shape)
    xo_ref[...] = xn
    if norm:
        ms = jnp.mean(xn * xn, axis=-1, keepdims=True)
        h_ref[...] = (xn * lax.rsqrt(ms + EPS) * g_ref[...]).astype(h_ref.dtype)


def _moe_combine(x, r_tab, slot, y3, next_g, tc=256):
    s, d = x.shape
    tc = _tile(s, tc)
    ns, nl = y3.shape[1:]
    norm = next_g is not None
    g = (next_g if norm else jnp.ones((d,), F32)).reshape(1, d)
    row = pl.BlockSpec((tc, d), lambda i, slot: (i, 0))
    out_specs = [row, row] if norm else [row]
    out_shape = [jax.ShapeDtypeStruct((s, d), F32)] + ([jax.ShapeDtypeStruct((s, d), BF16)] if norm else [])
    res = pl.pallas_call(
        functools.partial(_moe_combine_body, tc=tc, s=s, norm=norm),
        grid_spec=pltpu.PrefetchScalarGridSpec(
            num_scalar_prefetch=1,
            grid=(s // tc,),
            in_specs=[row, pl.BlockSpec((tc, LANES), lambda i, slot: (i, 0)), pl.BlockSpec((1, d), lambda i, slot: (0, 0)),
                      pl.BlockSpec(memory_space=pl.ANY)],
            out_specs=out_specs,
            scratch_shapes=[pltpu.VMEM((2, tc, ns, nl), F32), pltpu.VMEM((2, tc, ns, nl), F32), pltpu.SemaphoreType.DMA((2, 2))]),
        out_shape=out_shape,
        compiler_params=_params("arbitrary"),
        name="moe_combine",
    )(slot.reshape(2 * s), x, r_tab, g, y3)
    return (res[0], res[1]) if norm else (res[0], None)


def _attn_body(lq1_ref, lk1_ref, lq2_ref, lk2_ref, sg_ref, q_ref, k_ref, v_ref, o_ref,
               s_scr, p_scr, m_scr, l_scr, acc_scr, *, t, lam_init):
    dh = ATT_HEAD_DIM
    rs = ATT_STRIP
    i = pl.program_id(1)
    nt = (((1,), (1,)), ((), ()))

    m_scr[...] = jnp.full(m_scr.shape, -jnp.inf, F32)
    l_scr[...] = jnp.zeros(l_scr.shape, F32)
    acc_scr[...] = jnp.zeros(acc_scr.shape, F32)

    def chunk(j, masked):
        start = pl.multiple_of(j * t, t)
        tu = t // ATT_UNITS
        units = [(c, u) for u in range(ATT_UNITS) for c in range(2)]
        n_keys = lambda u: (u + 1) * tu if masked else t
        for c, u in units:
            cols = slice(c * dh, (c + 1) * dh)
            rows_u = slice(u * tu, (u + 1) * tu)
            s_scr[c, rows_u, 0:n_keys(u)] = lax.dot_general(q_ref[rows_u, cols], k_ref[pl.ds(start, n_keys(u)), cols], nt,
                                                           preferred_element_type=F32)
        for c, u in units:
            nk = n_keys(u)
            for r0 in range(u * tu, (u + 1) * tu, rs):
                rows = slice(r0, r0 + rs)
                s = s_scr[c, rows, 0:nk]
                if masked:
                    row = lax.broadcasted_iota(I32, (rs, nk), 0) + r0
                    col = lax.broadcasted_iota(I32, (rs, nk), 1)
                    s = jnp.where(col <= row, s, -jnp.inf)
                m_old = m_scr[c, rows, :]
                m_new = jnp.maximum(m_old, jnp.max(s, axis=-1, keepdims=True))
                alpha = jnp.exp2(m_old - m_new)
                p = jnp.exp2(s - jnp.tile(m_new, (1, nk // LANES)))
                l_scr[c, rows, :] = alpha * l_scr[c, rows, :] + jnp.sum(p, axis=-1, keepdims=True)
                m_scr[c, rows, :] = m_new
                acc_scr[c, rows, :] = acc_scr[c, rows, :] * jnp.tile(alpha, (1, 2 * dh // LANES))
                p_scr[c, rows, 0:nk] = p.astype(p_scr.dtype)
            rows_u = slice(u * tu, (u + 1) * tu)
            acc_scr[c, rows_u, :] += jnp.dot(p_scr[c, rows_u, 0:nk], v_ref[pl.ds(start, nk), :], preferred_element_type=F32)

    def body(j, _):
        chunk(j, False)
        return 0

    lax.fori_loop(0, i, body, 0)
    chunk(i, True)

    lam = (jnp.exp(jnp.sum(lq1_ref[...] * lk1_ref[...], axis=-1, keepdims=True))
           - jnp.exp(jnp.sum(lq2_ref[...] * lk2_ref[...], axis=-1, keepdims=True)) + lam_init)
    rep = (1, 2 * dh // LANES)
    o = acc_scr[0] / jnp.tile(l_scr[0], rep) - lam * (acc_scr[1] / jnp.tile(l_scr[1], rep))
    ms = jnp.mean(o * o, axis=-1, keepdims=True)
    y = o * lax.rsqrt(ms + EPS) * sg_ref[...] * (1.0 - lam_init)
    o_ref[...] = y.astype(o_ref.dtype)


def _diff_attention(proj, lq1, lk1, lq2, lk2, subln_g, n_heads, lam_init, t=1024):
    s = proj.shape[0]
    dh = ATT_HEAD_DIM
    hw = 2 * dh
    t = _tile(s, t)
    assert t % (ATT_UNITS * ATT_STRIP) == 0 and t % (ATT_UNITS * LANES) == 0
    vec = lambda h, i: (0, 0)
    once = pl.Buffered(1)
    return pl.pallas_call(
        functools.partial(_attn_body, t=t, lam_init=lam_init),
        grid=(n_heads, s // t),
        in_specs=[pl.BlockSpec((1, dh), vec), pl.BlockSpec((1, dh), vec), pl.BlockSpec((1, dh), vec),
                  pl.BlockSpec((1, dh), vec), pl.BlockSpec((1, hw), vec),
                  pl.BlockSpec((t, hw), lambda h, i: (i, h)),
                  pl.BlockSpec((s, hw), lambda h, i: (0, n_heads + h), pipeline_mode=once),
                  pl.BlockSpec((s, hw), lambda h, i: (0, 2 * n_heads + h), pipeline_mode=once)],
        out_specs=pl.BlockSpec((t, hw), lambda h, i: (i, h)),
        out_shape=jax.ShapeDtypeStruct((s, n_heads * hw), BF16),
        scratch_shapes=[pltpu.VMEM((2, t, t), F32), pltpu.VMEM((2, t, t), BF16), pltpu.VMEM((2, t, LANES), F32),
                        pltpu.VMEM((2, t, LANES), F32), pltpu.VMEM((2, t, hw), F32)],
        compiler_params=_params("parallel", "arbitrary"),
        name="diff_attention",
    )(lq1.reshape(1, dh), lk1.reshape(1, dh), lq2.reshape(1, dh), lk2.reshape(1, dh), subln_g.reshape(1, hw),
      proj, proj, proj)


def _tap_offsets(n_taps):
    return sorted({(CONV_HALO - (n_taps - 1) + j) % SUBLANES for j in range(n_taps)} - {0})


def _causal_dwconv(w_ref, src_ref, shift_ref, n_taps, ts, width, finish):
    offs = _tap_offsets(n_taps)
    span = CONV_HALO + ts - SUBLANES
    for k, off in enumerate(offs):
        shift_ref[k, 0:span, :] = src_ref[off:off + span, :]
    for r0 in range(0, ts, CONV_ROWS):
        acc = jnp.zeros((CONV_ROWS, width), F32)
        for j in range(n_taps):
            start = r0 + CONV_HALO - (n_taps - 1) + j
            off, base = start % SUBLANES, start - start % SUBLANES
            if off == 0:
                tap = src_ref[base:base + CONV_ROWS, :]
            else:
                tap = shift_ref[offs.index(off), base:base + CONV_ROWS, :]
            acc = acc + w_ref[j:j + 1, :] * tap
        finish(r0, acc)


def _conv_body(ga_ref, gb_ref, sb_ref, scc_ref, sh_ref, ga_h, gb_h, scc_h, sh_h,
               dw_ref, db_ref, lng_ref, lnb_ref, scw_ref, conf_ref, sc_ref, u_scr, p_scr, us_scr, ps_scr, *, ts):
    first = pl.program_id(0) == 0
    width = u_scr.shape[1]
    f32 = lambda r: r[...].astype(F32)
    u_scr[0:CONV_HALO, :] = jnp.where(first, 0.0, f32(ga_h) * jax.nn.sigmoid(f32(gb_h)))
    u_scr[CONV_HALO:CONV_HALO + ts, :] = f32(ga_ref) * jax.nn.sigmoid(f32(gb_ref))
    p_scr[0:CONV_HALO, :] = jnp.where(first, 0.0, f32(scc_h) * f32(sh_h))
    p_scr[CONV_HALO:CONV_HALO + ts, :] = f32(scc_ref) * f32(sh_ref)

    def finish_conf(r0, acc):
        u = acc + db_ref[...]
        mu = jnp.mean(u, axis=-1, keepdims=True)
        var = jnp.mean(jnp.square(u - mu), axis=-1, keepdims=True)
        y = (u - mu) * lax.rsqrt(var + EPS) * lng_ref[...] + lnb_ref[...]
        conf_ref[r0:r0 + CONV_ROWS, :] = (y * jax.nn.sigmoid(y)).astype(conf_ref.dtype)

    def finish_sc(r0, acc):
        sc_ref[r0:r0 + CONV_ROWS, :] = (sb_ref[r0:r0 + CONV_ROWS, :].astype(F32) * acc).astype(sc_ref.dtype)

    _causal_dwconv(dw_ref, u_scr, us_scr, CONF_KERNEL, ts, width, finish_conf)
    _causal_dwconv(scw_ref, p_scr, ps_scr, SC_KERNEL, ts, width, finish_sc)


def _conv_modules(proj, dw_w, dw_b, ln_g, ln_b, sc_w, width, col0, ts=256):
    s = proj.shape[0]
    ts = _tile(s, ts)
    assert ts % CONV_HALO == 0 and CONV_HALO >= CONF_KERNEL - 1
    hb = ts // CONV_HALO
    cur = lambda c: pl.BlockSpec((ts, width), lambda i: (i, col0 + c))
    halo = lambda c: pl.BlockSpec((CONV_HALO, width), lambda i: (jnp.maximum(i * hb - 1, 0), col0 + c))
    par = lambda n: pl.BlockSpec((n, width), lambda i: (0, 0))
    out = pl.BlockSpec((ts, width), lambda i: (i, 0))
    return pl.pallas_call(
        functools.partial(_conv_body, ts=ts),
        grid=(s // ts,),
        in_specs=[cur(0), cur(1), cur(2), cur(3), cur(4), halo(0), halo(1), halo(3), halo(4),
                  par(CONF_KERNEL), par(1), par(1), par(1), par(SC_KERNEL)],
        out_specs=[out, out],
        out_shape=[jax.ShapeDtypeStruct((s, width), BF16), jax.ShapeDtypeStruct((s, width), BF16)],
        scratch_shapes=[pltpu.VMEM((CONV_HALO + ts, width), F32), pltpu.VMEM((CONV_HALO + ts, width), F32),
                        pltpu.VMEM((len(_tap_offsets(CONF_KERNEL)), CONV_HALO + ts, width), F32),
                        pltpu.VMEM((len(_tap_offsets(SC_KERNEL)), CONV_HALO + ts, width), F32)],
        compiler_params=_params("parallel"),
        name="conv_modules",
    )(proj, proj, proj, proj, proj, proj, proj, proj, proj,
      dw_w, dw_b.reshape(1, width), ln_g.reshape(1, width), ln_b.reshape(1, width), sc_w)


def kernel(x, norm1_g, w_in, q_norm_g, k_norm_g, lam_q1, lam_k1, lam_q2, lam_k2, subln_g, conf_dw_w, conf_dw_b, conf_ln_g, conf_ln_b, sc_w, w_out, norm2_g, ffn_wg, ffn_wu, ffn_wd, router_w, moe_wg, moe_wu, moe_wd):
    b, s, d = x.shape
    depth = w_in.shape[0]
    dh = ATT_HEAD_DIM
    att_width = d // 2
    n_heads = att_width // (2 * dh)
    width = d // 4
    in_cols = w_in.shape[2]
    n_experts = router_w.shape[2]
    moe_tile = min(MOE_TILE, s)

    bf = lambda w: w.astype(BF16)
    w_in_b, w_out_b = bf(w_in), bf(w_out)
    ffn_wg_b, ffn_wu_b, ffn_wd_b = bf(ffn_wg), bf(ffn_wu), bf(ffn_wd)
    moe_up_tn = _tile(moe_wg.shape[3], MOE_UP_COLS)
    moe_down_tn = _tile(d, MOE_DOWN_COLS)
    moe_wg_b, moe_wu_b = _col_blocked(bf(moe_wg), moe_up_tn), _col_blocked(bf(moe_wu), moe_up_tn)
    moe_wd_b = _col_blocked(bf(moe_wd), moe_down_tn)

    ones_rest = jnp.ones((in_cols - 2 * att_width,), F32)
    qk_scale = jnp.concatenate([jnp.full((att_width,), dh ** -0.5 * LOG2E, F32), jnp.ones((in_cols - att_width,), F32)])

    outs = []
    for bi in range(b):
        xs = x[bi]
        h = None
        for l in range(depth):
            lam_init = 0.8 - 0.6 * math.exp(-0.3 * l)
            if h is None:
                h = _rmsnorm(xs, norm1_g[l])
            gain = jnp.concatenate([jnp.tile(q_norm_g[l], att_width // dh), jnp.tile(k_norm_g[l], att_width // dh), ones_rest])
            proj = _mm_in(h, w_in_b, l, gain.reshape(1, in_cols), qk_scale.reshape(1, in_cols), 2 * att_width)
            att = _diff_attention(proj, lam_q1[l], lam_k1[l], lam_q2[l], lam_k2[l], subln_g[l], n_heads, lam_init)
            conf, sc = _conv_modules(proj, conf_dw_w[l], conf_dw_b[l], conf_ln_g[l], conf_ln_b[l], sc_w[l],
                                     width, 3 * att_width // width)
            mix = jnp.concatenate([att, conf, sc], axis=-1)
            xs = _mm_resid(mix, w_out_b, l, xs)
            i = l // 2
            if l % 2 == 0:
                h = _rmsnorm(xs, norm2_g[l])
                hid = _mm_gated(h, ffn_wg_b, ffn_wu_b, i)
                xs = _mm_resid(hid, ffn_wd_b, i, xs)
                h = None
            else:
                h3, r_tab = _rmsnorm_router(xs, norm2_g[l], router_w[i])
                src, slot, tile_expert, n_live = _route(r_tab, n_experts, moe_tile)
                hid = _moe_up(h3, src, tile_expert, n_live, moe_wg_b, moe_wu_b, i, moe_tile)
                y3 = _moe_down(hid, tile_expert, n_live, moe_wd_b, i, moe_tile)
                xs, h = _moe_combine(xs, r_tab, slot, y3, norm1_g[l + 1] if l + 1 < depth else None)
        outs.append(xs)
    return jnp.stack(outs)
```
